```python
import jax
import jax.numpy as jnp
from jax import lax
import numpy as np

D_MODEL = 1024
BATCH = 8
SEQ = 4096
DEPTH = 2

GRID_W = 64
CTX_LEN = 256
EPS = 1e-6
NEG_INF = -1e30
N_MOD = 6

MLA_HEADS = 8
MLA_Q_LORA = 384
MLA_KV_LORA = 256
MLA_NOPE = 64
MLA_ROPE = 32
MLA_V = 64
MLA_QK = MLA_NOPE + MLA_ROPE
MLA_WIDTH = MLA_HEADS * MLA_V
ROPE_BASE = 10000.0
Q_BLOCK = 128

NA_HEADS = 4
NA_HEAD_DIM = 64
NA_WIDTH = NA_HEADS * NA_HEAD_DIM
NA_WIN_ROWS = 8
NA_WIN_COLS = 16
NA_COL_BLOCK = 16
NA_COL_SPAN = NA_COL_BLOCK + NA_WIN_COLS

CM_GROUPS = 4
CM_CHUNK = 128
CM_WIDTH = 256
CM_GROUP_DIM = CM_WIDTH // CM_GROUPS

N_BRANCH = 3
MIX_WIDTH = MLA_WIDTH + NA_WIDTH + CM_WIDTH
IN_SPLITS = (MLA_Q_LORA, MLA_KV_LORA, MLA_ROPE, NA_WIDTH, NA_WIDTH, NA_WIDTH, CM_WIDTH, CM_WIDTH, N_BRANCH * D_MODEL)
IN_WIDTH = MLA_Q_LORA + MLA_KV_LORA + MLA_ROPE + 3 * NA_WIDTH + 2 * CM_WIDTH + N_BRANCH * D_MODEL

D_FF = 2816
N_EXPERTS = 8
TOP_K = 2
D_FF_EXPERT = 3584
MOE_BLOCK = 512
N_DENSE = (DEPTH + 1) // 2
N_MOE = DEPTH // 2

kernel_name = 'hybrid_mla_natten_chunkmlp_moe_dit'


def rms_norm(x, g):
    xf = x.astype(jnp.float32)
    y = xf * lax.rsqrt(jnp.mean(xf * xf, axis=-1, keepdims=True) + EPS)
    return (y * g.astype(jnp.float32)).astype(x.dtype)


def layer_norm(x, g, b):
    xf = x.astype(jnp.float32)
    mu = jnp.mean(xf, axis=-1, keepdims=True)
    xc = xf - mu
    y = xc * lax.rsqrt(jnp.mean(xc * xc, axis=-1, keepdims=True) + EPS)
    return (y * g.astype(jnp.float32) + b.astype(jnp.float32)).astype(x.dtype)


def split_in(p):
    return jnp.split(p, np.cumsum(IN_SPLITS)[:-1], axis=-1)


def axial_rope(rows, dtype):
    half = MLA_ROPE // 2
    inv = ROPE_BASE ** (-(jnp.arange(0, half, 2, dtype=jnp.float32) / half))
    t = jnp.arange(rows * GRID_W)
    ang_r = (t // GRID_W).astype(jnp.float32)[:, None] * inv
    ang_c = (t % GRID_W).astype(jnp.float32)[:, None] * inv
    ang = jnp.concatenate([ang_r, ang_r, ang_c, ang_c], axis=-1)[:, None, :]
    return jnp.cos(ang).astype(dtype), jnp.sin(ang).astype(dtype)


def rope_2d(x, cos, sin):
    half = MLA_ROPE // 2
    quarter = half // 2
    def rot(z):
        return jnp.concatenate([-z[..., quarter:], z[..., :quarter]], axis=-1)
    rotated = jnp.concatenate([rot(x[..., :half]), rot(x[..., half:])], axis=-1)
    return x * cos + rotated * sin


def mla_q(cq, g_q, w_uq, cos, sin):
    b, l, _ = cq.shape
    q = (rms_norm(cq, g_q) @ w_uq).reshape(b, l, MLA_HEADS, MLA_QK)
    q_nope, q_rope = q[..., :MLA_NOPE], q[..., MLA_NOPE:]
    if cos is not None:
        q_rope = rope_2d(q_rope, cos, sin)
    return jnp.concatenate([q_nope, q_rope], axis=-1)


def mla_kv(ckv, k_rope, g_kv, w_ukv, cos, sin):
    b, l, _ = ckv.shape
    kv = (rms_norm(ckv, g_kv) @ w_ukv).reshape(b, l, MLA_HEADS, MLA_NOPE + MLA_V)
    k_nope, v = kv[..., :MLA_NOPE], kv[..., MLA_NOPE:]
    kr = k_rope[:, :, None, :]
    if cos is not None:
        kr = rope_2d(kr, cos, sin)
    k = jnp.concatenate([k_nope, jnp.broadcast_to(kr, (b, l, MLA_HEADS, MLA_ROPE))], axis=-1)
    return k, v


def dense_attention(q, k, v):
    s = jnp.einsum('bqhd,bkhd->bhqk', q, k).astype(jnp.float32) * (q.shape[-1] ** -0.5)
    p = jax.nn.softmax(s, axis=-1).astype(v.dtype)
    o = jnp.einsum('bhqk,bkhd->bqhd', p, v)
    return o.reshape(o.shape[0], o.shape[1], -1)


def mla_latent_attention(q, k_all, v_all):
    b, s_len, h, dq = q.shape
    nb = s_len // Q_BLOCK
    qb = q.reshape(b, nb, Q_BLOCK, h, dq).transpose(1, 0, 2, 3, 4)
    scale = dq ** -0.5
    def block(qi):
        s = jnp.einsum('bqhd,bkhd->bhqk', qi, k_all).astype(jnp.float32) * scale
        p = jax.nn.softmax(s, axis=-1).astype(v_all.dtype)
        return jnp.einsum('bhqk,bkhd->bqhd', p, v_all)
    o = lax.map(block, qb)
    return o.transpose(1, 0, 2, 3, 4).reshape(b, s_len, h * MLA_V)


def na_latent_attention(q, k, v, k_ctx, v_ctx, rpb):
    b, s_len, h, hd = q.shape
    rows = s_len // GRID_W
    kr = min(NA_WIN_ROWS, rows)
    nblk = GRID_W // NA_COL_BLOCK
    scale = hd ** -0.5
    qg = q.reshape(b, rows, nblk, NA_COL_BLOCK, h, hd)
    kg = k.reshape(b, rows, GRID_W, h, hd)
    vg = v.reshape(b, rows, GRID_W, h, hd)
    blk = np.arange(nblk)
    span_start = np.clip(blk * NA_COL_BLOCK - NA_WIN_COLS // 2, 0, GRID_W - NA_COL_SPAN)
    span_cols = span_start[:, None] + np.arange(NA_COL_SPAN)
    q_cols = blk[:, None] * NA_COL_BLOCK + np.arange(NA_COL_BLOCK)
    win_start = np.clip(q_cols - NA_WIN_COLS // 2, 0, GRID_W - NA_WIN_COLS)
    col_ok = jnp.asarray((span_cols[:, None, :] >= win_start[:, :, None]) & (span_cols[:, None, :] < win_start[:, :, None] + NA_WIN_COLS))
    dc_idx = jnp.asarray(np.clip(span_cols[:, None, :] - q_cols[:, :, None] + NA_WIN_COLS - 1, 0, 2 * NA_WIN_COLS - 2))
    n_win = kr * NA_COL_SPAN
    def row_step(r):
        rs = jnp.clip(r - NA_WIN_ROWS // 2, 0, rows - kr)
        kb = lax.dynamic_slice_in_dim(kg, rs, kr, axis=1)[:, :, span_cols]
        vb = lax.dynamic_slice_in_dim(vg, rs, kr, axis=1)[:, :, span_cols]
        qr = lax.dynamic_index_in_dim(qg, r, axis=1, keepdims=False)
        s_win = jnp.einsum('bmqhd,bamkhd->bhmqak', qr, kb).astype(jnp.float32) * scale
        ri = rs + jnp.arange(kr) - r + NA_WIN_ROWS - 1
        bias = rpb[:, ri[None, None, :, None], dc_idx[:, :, None, :]]
        s_win = jnp.where(col_ok[:, :, None, :], s_win + bias[None].astype(jnp.float32), NEG_INF)
        s_ctx = jnp.einsum('bmqhd,bchd->bhmqc', qr, k_ctx).astype(jnp.float32) * scale
        s = jnp.concatenate([s_win.reshape(b, h, nblk, NA_COL_BLOCK, n_win), s_ctx], axis=-1)
        p = jax.nn.softmax(s, axis=-1).astype(v.dtype)
        p_win = p[..., :n_win].reshape(b, h, nblk, NA_COL_BLOCK, kr, NA_COL_SPAN)
        return (jnp.einsum('bhmqak,bamkhd->bmqhd', p_win, vb)
                + jnp.einsum('bhmqc,bchd->bmqhd', p[..., n_win:], v_ctx))
    o = lax.map(row_step, jnp.arange(rows))
    return o.transpose(1, 0, 2, 3, 4, 5).reshape(b, s_len, h * hd)


def chunk_mlp(u, v, g_v, b_v, w_sp, b_sp):
    b, l, _ = v.shape
    vn = layer_norm(v, g_v, b_v).reshape(b, l // CM_CHUNK, CM_CHUNK, CM_GROUPS, CM_GROUP_DIM)
    s = jnp.einsum('gpq,bnqgc->bnpgc', w_sp, vn) + b_sp.T[None, None, :, :, None]
    return u * s.reshape(b, l, CM_WIDTH)


def merge_branches(y_mla, y_na, y_cm, gate_logits, w_br, w_out):
    b, l, _ = y_mla.shape
    g = jax.nn.sigmoid(gate_logits.astype(jnp.float32)).astype(y_mla.dtype).reshape(b, l, N_BRANCH, D_MODEL)
    o1, o2 = MLA_WIDTH, MLA_WIDTH + NA_WIDTH
    z = (g[:, :, 0] * (y_mla @ w_br[:o1]) + g[:, :, 1] * (y_na @ w_br[o1:o2]) + g[:, :, 2] * (y_cm @ w_br[o2:]))
    return z @ w_out


def token_mixer(n_x, n_c, cos, sin, with_ctx_out, w_in, g_q, w_uq, g_kv, w_ukv, rpb, g_v, b_v, w_sp, b_sp, w_br, w_out):
    def heads(t):
        return t.reshape(t.shape[0], t.shape[1], NA_HEADS, NA_HEAD_DIM)
    cq_x, ckv_x, kr_x, naq_x, nak_x, nav_x, u_x, v_x, gl_x = split_in(n_x @ w_in)
    cq_c, ckv_c, kr_c, naq_c, nak_c, nav_c, u_c, v_c, gl_c = split_in(n_c @ w_in)
    k_x, vv_x = mla_kv(ckv_x, kr_x, g_kv, w_ukv, cos, sin)
    k_c, vv_c = mla_kv(ckv_c, kr_c, g_kv, w_ukv, None, None)
    q_x = mla_q(cq_x, g_q, w_uq, cos, sin)
    y_mla_x = mla_latent_attention(q_x, jnp.concatenate([k_c, k_x], axis=1), jnp.concatenate([vv_c, vv_x], axis=1))
    y_na_x = na_latent_attention(heads(naq_x), heads(nak_x), heads(nav_x), heads(nak_c), heads(nav_c), rpb)
    y_cm_x = chunk_mlp(u_x, v_x, g_v, b_v, w_sp, b_sp)
    out_x = merge_branches(y_mla_x, y_na_x, y_cm_x, gl_x, w_br, w_out)
    if not with_ctx_out:
        return out_x, None
    q_c = mla_q(cq_c, g_q, w_uq, None, None)
    y_mla_c = dense_attention(q_c, k_c, vv_c)
    y_na_c = dense_attention(heads(naq_c), heads(nak_c), heads(nav_c))
    y_cm_c = chunk_mlp(u_c, v_c, g_v, b_v, w_sp, b_sp)
    out_c = merge_branches(y_mla_c, y_na_c, y_cm_c, gl_c, w_br, w_out)
    return out_x, out_c


def swiglu(t, w1, w3, w2):
    return (jax.nn.silu(t @ w1) * (t @ w3)) @ w2


def moe_swiglu(t, w_router, w1, w3, w2):
    b, l, d = t.shape
    n_tok = b * l
    tf = t.reshape(n_tok, d)
    logits = (tf @ w_router).astype(jnp.float32)
    top_logit, top_e = lax.top_k(logits, TOP_K)
    top_w = jax.nn.softmax(top_logit, axis=-1).astype(t.dtype)
    n_asg = n_tok * TOP_K
    e_flat = top_e.reshape(n_asg)
    tok_flat = jnp.arange(n_asg) // TOP_K
    order = jnp.argsort(e_flat)
    e_s, tok_s, w_s = e_flat[order], tok_flat[order], top_w.reshape(n_asg)[order]
    counts = jnp.bincount(e_flat, length=N_EXPERTS)
    starts = jnp.cumsum(counts) - counts
    padded = (counts + MOE_BLOCK - 1) // MOE_BLOCK * MOE_BLOCK
    pad_end = jnp.cumsum(padded)
    pad_start = pad_end - padded
    dest = pad_start[e_s] + jnp.arange(n_asg) - starts[e_s]
    n_blocks = -(-n_asg // MOE_BLOCK) + N_EXPERTS
    n_pad = n_blocks * MOE_BLOCK
    pad_tok = jnp.full((n_pad,), n_tok, jnp.int32).at[dest].set(tok_s)
    pad_w = jnp.zeros((n_pad,), t.dtype).at[dest].set(w_s)
    blk_expert = jnp.minimum(jnp.searchsorted(pad_end, jnp.arange(n_blocks) * MOE_BLOCK, side='right'), N_EXPERTS - 1)
    x_pad = jnp.concatenate([tf, jnp.zeros((1, d), t.dtype)], axis=0)[pad_tok].reshape(n_blocks, MOE_BLOCK, d)
    def expert_block(args):
        xb, e = args
        return swiglu(xb, w1[e], w3[e], w2[e])
    y = lax.map(expert_block, (x_pad, blk_expert)).reshape(n_pad, d) * pad_w[:, None]
    out = jax.ops.segment_sum(y, pad_tok, num_segments=n_tok + 1)[:n_tok]
    return out.reshape(b, l, d)


def channel_mixer(t, layer, w_ffn1, w_ffn3, w_ffn2, w_router, w_moe1, w_moe3, w_moe2):
    i = layer // 2
    if layer % 2 == 0:
        return swiglu(t, w_ffn1[i], w_ffn3[i], w_ffn2[i])
    return moe_swiglu(t, w_router[i], w_moe1[i], w_moe3[i], w_moe2[i])


def setup_inputs(seed: int = 0) -> dict:
    key = jax.random.key(seed)
    ks = iter(jax.random.split(key, 32))
    def nrm(shape, scale):
        return jax.random.normal(next(ks), shape, jnp.float32) * scale
    def gain(shape):
        return 1.0 + nrm(shape, 0.02)
    L = DEPTH
    return {
        'x': nrm((BATCH, SEQ, D_MODEL), 1.0),
        'c': nrm((BATCH, D_MODEL), 1.0),
        'ctx': nrm((BATCH, CTX_LEN, D_MODEL), 1.0),
        'c_ctx': nrm((D_MODEL,), 1.0),
        'w_mod': nrm((L, D_MODEL, N_MOD * D_MODEL), D_MODEL ** -0.5),
        'b_mod': nrm((L, N_MOD * D_MODEL), 0.02),
        'g_pre_mix': gain((L, D_MODEL)),
        'g_post_mix': gain((L, D_MODEL)),
        'g_pre_ffn': gain((L, D_MODEL)),
        'g_post_ffn': gain((L, D_MODEL)),
        'w_in': nrm((L, D_MODEL, IN_WIDTH), D_MODEL ** -0.5),
        'g_q': gain((L, MLA_Q_LORA)),
        'w_uq': nrm((L, MLA_Q_LORA, MLA_HEADS * MLA_QK), MLA_Q_LORA ** -0.5),
        'g_kv': gain((L, MLA_KV_LORA)),
        'w_ukv': nrm((L, MLA_KV_LORA, MLA_HEADS * (MLA_NOPE + MLA_V)), MLA_KV_LORA ** -0.5),
        'rpb': nrm((L, NA_HEADS, 2 * NA_WIN_ROWS - 1, 2 * NA_WIN_COLS - 1), 0.1),
        'g_v': gain((L, CM_WIDTH)),
        'b_v': nrm((L, CM_WIDTH), 0.02),
        'w_sp': nrm((L, CM_GROUPS, CM_CHUNK, CM_CHUNK), CM_CHUNK ** -0.5),
        'b_sp': nrm((L, CM_GROUPS, CM_CHUNK), 0.02),
        'w_br': nrm((L, MIX_WIDTH, D_MODEL), D_MODEL ** -0.5),
        'w_out': nrm((L, D_MODEL, D_MODEL), D_MODEL ** -0.5),
        'w_ffn1': nrm((N_DENSE, D_MODEL, D_FF), D_MODEL ** -0.5),
        'w_ffn3': nrm((N_DENSE, D_MODEL, D_FF), D_MODEL ** -0.5),
        'w_ffn2': nrm((N_DENSE, D_FF, D_MODEL), D_FF ** -0.5),
        'w_router': nrm((N_MOE, D_MODEL, N_EXPERTS), D_MODEL ** -0.5),
        'w_moe1': nrm((N_MOE, N_EXPERTS, D_MODEL, D_FF_EXPERT), D_MODEL ** -0.5),
        'w_moe3': nrm((N_MOE, N_EXPERTS, D_MODEL, D_FF_EXPERT), D_MODEL ** -0.5),
        'w_moe2': nrm((N_MOE, N_EXPERTS, D_FF_EXPERT, D_MODEL), D_FF_EXPERT ** -0.5),
    }


def reference(x, c, ctx, c_ctx, w_mod, b_mod, g_pre_mix, g_post_mix, g_pre_ffn, g_post_ffn, w_in, g_q, w_uq, g_kv, w_ukv, rpb, g_v, b_v, w_sp, b_sp, w_br, w_out, w_ffn1, w_ffn3, w_ffn2, w_router, w_moe1, w_moe3, w_moe2):
    b, s_len, d = x.shape
    rows = s_len // GRID_W
    cos, sin = axial_rope(rows, x.dtype)
    sc = jax.nn.silu(c)
    scc = jax.nn.silu(c_ctx)
    h_x, h_c = x, ctx
    for l in range(DEPTH):
        last = l == DEPTH - 1
        mod_x = (sc @ w_mod[l] + b_mod[l]).reshape(b, N_MOD, 1, d)
        mod_c = (scc @ w_mod[l] + b_mod[l]).reshape(N_MOD, d)
        n_x = rms_norm(h_x, g_pre_mix[l]) * (1 + mod_x[:, 1]) + mod_x[:, 0]
        n_c = rms_norm(h_c, g_pre_mix[l]) * (1 + mod_c[1]) + mod_c[0]
        o_x, o_c = token_mixer(n_x, n_c, cos, sin, not last, w_in[l], g_q[l], w_uq[l], g_kv[l], w_ukv[l], rpb[l], g_v[l], b_v[l], w_sp[l], b_sp[l], w_br[l], w_out[l])
        h_x = h_x + mod_x[:, 2] * rms_norm(o_x, g_post_mix[l])
        f_x = channel_mixer(rms_norm(h_x, g_pre_ffn[l]) * (1 + mod_x[:, 4]) + mod_x[:, 3], l, w_ffn1, w_ffn3, w_ffn2, w_router, w_moe1, w_moe3, w_moe2)
        h_x = h_x + mod_x[:, 5] * rms_norm(f_x, g_post_ffn[l])
        if not last:
            h_c = h_c + mod_c[2] * rms_norm(o_c, g_post_mix[l])
            f_c = channel_mixer(rms_norm(h_c, g_pre_ffn[l]) * (1 + mod_c[4]) + mod_c[3], l, w_ffn1, w_ffn3, w_ffn2, w_router, w_moe1, w_moe3, w_moe2)
            h_c = h_c + mod_c[5] * rms_norm(f_c, g_post_ffn[l])
    return h_x
```

```python
import functools

import numpy as np
import jax
import jax.numpy as jnp
from jax import lax
from jax.experimental import pallas as pl
from jax.experimental.pallas import tpu as pltpu

F32 = jnp.float32
BF16 = jnp.bfloat16

D = 1024
B = 8
S = 4096
DEPTH = 2
GRID_W = 64
ROWS = S // GRID_W
C = 256
EPS = 1e-6
NEG = -1e30
N_MOD = 6

H_MLA = 8
Q_LORA = 384
KV_LORA = 256
NOPE = 64
ROPE = 32
VD = 64
QK = NOPE + ROPE
ROPE_BASE = 10000.0

H_NA = 4
HD_NA = 64
NA_W = H_NA * HD_NA
WIN_R = 8
WIN_C = 16
NA_QROWS = 4
NA_KROWS = 12

CM_G = 4
CM_CHUNK = 128
CM_W = 256

D_FF = 2816
N_EXP = 8
D_FFE = 3584
MOE_BLK = 512

NX = B * S
NC = B * C
N = NX + NC
LANE = 128

TM = 512
TQ = 256
TK = 512
FT = 512
N_ASG = NX * 2
N_BLK = N_ASG // MOE_BLK + N_EXP
N_PAD = N_BLK * MOE_BLK

O_CQ, O_CKV, O_KRA, O_KRB, O_NAQ, O_NAK, O_NAV, O_U, O_V, O_GL = 0, 384, 640, 768, 896, 1152, 1408, 1664, 1920, 2176
W_IN_PACKED = O_GL + 3 * D

VMEM_BIG = 56 * 1024 * 1024


def _rms(xf, g):
    return xf * lax.rsqrt(jnp.mean(xf * xf, axis=-1, keepdims=True) + EPS) * g


def _dot(a, b):
    return jnp.dot(a, b, preferred_element_type=F32)


def _dot_nt(a, b):
    return lax.dot_general(a, b, (((1,), (1,)), ((), ())), preferred_element_type=F32)


def _const_spec(shape):
    nd = len(shape)
    return pl.BlockSpec(shape, lambda *_: (0,) * nd, pipeline_mode=pl.Buffered(1))


def _mod_index(tiles_per_batch):
    return lambda i: (jnp.minimum(i // tiles_per_batch, B), 0, 0)


def _mod_kernel(c_ref, w_ref, b_ref, o_ref):
    c = c_ref[...]
    sc = c * jax.nn.sigmoid(c)
    o_ref[0] = jnp.dot(sc, w_ref[0], preferred_element_type=F32, precision=lax.Precision.HIGHEST) + b_ref[0]


def _modulation(c_all, w_mod, b_mod):
    tn = 1536
    return pl.pallas_call(
        _mod_kernel,
        grid=(DEPTH, N_MOD * D // tn),
        in_specs=[
            pl.BlockSpec((16, D), lambda l, j: (0, 0)),
            pl.BlockSpec((1, D, tn), lambda l, j: (l, 0, j)),
            pl.BlockSpec((1, 1, tn), lambda l, j: (l, 0, j)),
        ],
        out_specs=pl.BlockSpec((1, 16, tn), lambda l, j: (l, 0, j)),
        out_shape=jax.ShapeDtypeStruct((DEPTH, 16, N_MOD * D), F32),
        compiler_params=pltpu.CompilerParams(dimension_semantics=("parallel", "parallel"), vmem_limit_bytes=VMEM_BIG),
        name="modulation",
    )(c_all, w_mod, b_mod.reshape(DEPTH, 1, N_MOD * D))


def _inproj_kernel(h_ref, mod_ref, gpre_ref, win_ref, gq_ref, wqa_ref, wqb_ref, gkv_ref, wuk_ref, wuv_ref,
                   cos_ref, sin_ref, gv_ref, bv_ref, wsp_ref, bsp_ref,
                   q_ref, k_ref, v_ref, naq_ref, nak_ref, nav_ref, ycm_ref, sg_ref):
    n = _rms(h_ref[...], gpre_ref[...]) * (1.0 + mod_ref[0, 1:2, :]) + mod_ref[0, 0:1, :]
    nb = n.astype(BF16)

    def proj(a, b):
        return _dot(nb, win_ref[:, a:b])

    cos = cos_ref[...]
    sin = sin_ref[...]

    cqn = _rms(proj(O_CQ, O_CKV), gq_ref[...]).astype(BF16)
    qa = _dot(cqn, wqa_ref[...])
    qb = _dot(cqn, wqb_ref[...])
    for hh in range(H_MLA):
        sl = slice(hh * LANE, (hh + 1) * LANE)
        q_ref[hh] = ((qa[:, sl] * cos + qb[:, sl] * sin) * (QK ** -0.5)).astype(BF16)
    ckvn = _rms(proj(O_CKV, O_KRA), gkv_ref[...]).astype(BF16)
    krp = proj(O_KRA, O_KRB) * cos + proj(O_KRB, O_NAQ) * sin
    kk = _dot(ckvn, wuk_ref[...])
    for hh in range(H_MLA):
        k_ref[hh] = (kk[:, hh * LANE:(hh + 1) * LANE] + krp).astype(BF16)
    vv = _dot(ckvn, wuv_ref[...])
    for p in range(H_MLA // 2):
        v_ref[p] = vv[:, p * LANE:(p + 1) * LANE].astype(BF16)

    naq = proj(O_NAQ, O_NAK) * (HD_NA ** -0.5)
    nak = proj(O_NAK, O_NAV)
    nav = proj(O_NAV, O_U)
    for p in range(H_NA // 2):
        sl = slice(p * LANE, (p + 1) * LANE)
        naq_ref[p] = naq[:, sl].astype(BF16)
        nak_ref[p] = nak[:, sl].astype(BF16)
        nav_ref[p] = nav[:, sl].astype(BF16)

    u = proj(O_U, O_V)
    v = proj(O_V, O_GL)
    mu = jnp.mean(v, axis=-1, keepdims=True)
    vc = v - mu
    vn = (vc * lax.rsqrt(jnp.mean(vc * vc, axis=-1, keepdims=True) + EPS) * gv_ref[...] + bv_ref[...]).astype(BF16)
    grp = lax.broadcasted_iota(jnp.int32, (CM_CHUNK, CM_W), 1) // (CM_W // CM_G)
    for ch in range(TM // CM_CHUNK):
        rs = slice(ch * CM_CHUNK, (ch + 1) * CM_CHUNK)
        vch = vn[rs, :]
        sp = _dot(wsp_ref[CM_G - 1], vch)
        for g in range(CM_G - 2, -1, -1):
            sp = jnp.where(grp == g, _dot(wsp_ref[g], vch), sp)
        ycm_ref[rs, :] = (u[rs, :] * (sp + bsp_ref[...])).astype(BF16)

    for j in range(3):
        sg_ref[:, j * D:(j + 1) * D] = jax.nn.sigmoid(proj(O_GL + j * D, O_GL + (j + 1) * D)).astype(BF16)


def _inproj(h, mod_l, gpre, wl, rope_cos, rope_sin):
    tiles_x = S // TM
    n_tiles = N // TM
    pos_idx = lambda i: (jnp.where(i < NX // TM, i % tiles_x, tiles_x), 0)
    row = lambda i: (i, 0)
    hrow = lambda i: (0, i, 0)
    in_specs = [
        pl.BlockSpec((TM, D), row),
        pl.BlockSpec((1, N_MOD, D), _mod_index(tiles_x)),
        _const_spec((1, D)),
        _const_spec((D, W_IN_PACKED)),
        _const_spec((1, Q_LORA)),
        _const_spec((Q_LORA, H_MLA * LANE)),
        _const_spec((Q_LORA, H_MLA * LANE)),
        _const_spec((1, KV_LORA)),
        _const_spec((KV_LORA, H_MLA * LANE)),
        _const_spec((KV_LORA, H_MLA * VD)),
        pl.BlockSpec((TM, LANE), pos_idx),
        pl.BlockSpec((TM, LANE), pos_idx),
        _const_spec((1, CM_W)),
        _const_spec((1, CM_W)),
        _const_spec((CM_G, CM_CHUNK, CM_CHUNK)),
        _const_spec((CM_CHUNK, CM_W)),
    ]
    out_shape = [
        jax.ShapeDtypeStruct((H_MLA, N, LANE), BF16),
        jax.ShapeDtypeStruct((H_MLA, N, LANE), BF16),
        jax.ShapeDtypeStruct((H_MLA // 2, N, LANE), BF16),
        jax.ShapeDtypeStruct((H_NA // 2, N, LANE), BF16),
        jax.ShapeDtypeStruct((H_NA // 2, N, LANE), BF16),
        jax.ShapeDtypeStruct((H_NA // 2, N, LANE), BF16),
        jax.ShapeDtypeStruct((N, CM_W), BF16),
        jax.ShapeDtypeStruct((N, 3 * D), BF16),
    ]
    out_specs = [
        pl.BlockSpec((H_MLA, TM, LANE), hrow),
        pl.BlockSpec((H_MLA, TM, LANE), hrow),
        pl.BlockSpec((H_MLA // 2, TM, LANE), hrow),
        pl.BlockSpec((H_NA // 2, TM, LANE), hrow),
        pl.BlockSpec((H_NA // 2, TM, LANE), hrow),
        pl.BlockSpec((H_NA // 2, TM, LANE), hrow),
        pl.BlockSpec((TM, CM_W), row),
        pl.BlockSpec((TM, 3 * D), row),
    ]
    return pl.pallas_call(
        _inproj_kernel,
        grid=(n_tiles,),
        in_specs=in_specs,
        out_specs=out_specs,
        out_shape=out_shape,
        compiler_params=pltpu.CompilerParams(dimension_semantics=("parallel",), vmem_limit_bytes=VMEM_BIG),
        name="inproj",
    )(h, mod_l, gpre, wl["w_in"], wl["g_q"], wl["w_qa"], wl["w_qb"], wl["g_kv"], wl["w_uk"], wl["w_uv"],
      rope_cos, rope_sin, wl["g_v"], wl["b_v"], wl["w_sp"], wl["b_sp"])


def _flash_update(q, kc, vc, m_scr, l_scr, acc_scr):
    s = _dot_nt(q, kc)
    m_prev = m_scr[...]
    m_new = jnp.maximum(m_prev, jnp.max(s, axis=-1, keepdims=True))
    alpha = jnp.exp(m_prev - m_new)
    p = jnp.exp(s - m_new)
    l_scr[...] = alpha * l_scr[...] + jnp.sum(p, axis=-1, keepdims=True)
    acc_scr[...] = alpha * acc_scr[...] + _dot(p.astype(BF16), vc)
    m_scr[...] = m_new


def _mla_kernel(q_ref, kx_ref, kc_ref, vx_ref, vc_ref, o_ref, m_scr, l_scr, acc_scr, *, n_qx):
    is_x = pl.program_id(2) < n_qx
    n_chunks = jnp.where(is_x, S // TK, 0)
    for hh in range(2):
        q = q_ref[hh]
        m_scr[...] = jnp.full(m_scr.shape, NEG, F32)
        l_scr[...] = jnp.zeros(l_scr.shape, F32)
        acc_scr[...] = jnp.zeros(acc_scr.shape, F32)
        _flash_update(q, kc_ref[hh], vc_ref[0], m_scr, l_scr, acc_scr)

        def body(c, carry):
            ks = pl.ds(pl.multiple_of(c * TK, TK), TK)
            _flash_update(q, kx_ref[hh, ks, :], vx_ref[0, ks, :], m_scr, l_scr, acc_scr)
            return carry

        lax.fori_loop(0, n_chunks, body, 0)
        sl = slice(hh * VD, (hh + 1) * VD)
        o_ref[:, sl] = (acc_scr[:, sl] / l_scr[...]).astype(o_ref.dtype)


def _mla_attention(q, k, v, with_ctx):
    n_qx = S // TQ
    n_q = n_qx + (1 if with_ctx else 0)
    xq = NX // TQ
    qrow = lambda b, q_: jnp.where(q_ < n_qx, b * n_qx + q_, xq + b)
    return pl.pallas_call(
        functools.partial(_mla_kernel, n_qx=n_qx),
        grid=(B, H_MLA // 2, n_q),
        in_specs=[
            pl.BlockSpec((2, TQ, LANE), lambda b, p, q_: (p, qrow(b, q_), 0)),
            pl.BlockSpec((2, S, LANE), lambda b, p, q_: (p, b, 0)),
            pl.BlockSpec((2, C, LANE), lambda b, p, q_: (p, NX // C + b, 0)),
            pl.BlockSpec((1, S, LANE), lambda b, p, q_: (p, b, 0)),
            pl.BlockSpec((1, C, LANE), lambda b, p, q_: (p, NX // C + b, 0)),
        ],
        out_specs=pl.BlockSpec((TQ, LANE), lambda b, p, q_: (qrow(b, q_), p)),
        out_shape=jax.ShapeDtypeStruct((N, H_MLA * VD), BF16),
        scratch_shapes=[pltpu.VMEM((TQ, 1), F32), pltpu.VMEM((TQ, 1), F32), pltpu.VMEM((TQ, LANE), F32)],
        compiler_params=pltpu.CompilerParams(dimension_semantics=("parallel", "parallel", "arbitrary"),
                                             vmem_limit_bytes=VMEM_BIG),
        name="mla_attention",
    )(q, k, k, v, v)


def _na_kernel(q_ref, kx_ref, kc_ref, vx_ref, vc_ref, bias_ref, o_ref, *, n_g):
    g = pl.program_id(1)
    lane = lax.broadcasted_iota(jnp.int32, (TQ, LANE), 1)

    def head_q(p, half):
        qp = q_ref[p]
        return jnp.where((lane < HD_NA) == (half == 0), qp, jnp.zeros_like(qp))

    def finish(p, outs):
        o_ref[:, p * LANE:(p + 1) * LANE] = jnp.where(lane < HD_NA, outs[0], outs[1]).astype(o_ref.dtype)

    @pl.when(g < n_g)
    def _():
        base = jnp.clip(g * NA_QROWS - WIN_R // 2, 0, ROWS - NA_KROWS)
        ks = pl.ds(pl.multiple_of(base * GRID_W, GRID_W), NA_KROWS * GRID_W)
        for p in range(H_NA // 2):
            kw = kx_ref[p, ks, :]
            vw = vx_ref[p, ks, :]
            outs = []
            for half in range(2):
                q = head_q(p, half)
                s_w = _dot_nt(q, kw) + bias_ref[0, 2 * p + half]
                s_c = _dot_nt(q, kc_ref[p])
                m = jnp.maximum(jnp.max(s_w, axis=-1, keepdims=True), jnp.max(s_c, axis=-1, keepdims=True))
                p_w = jnp.exp(s_w - m)
                p_c = jnp.exp(s_c - m)
                l = jnp.sum(p_w, axis=-1, keepdims=True) + jnp.sum(p_c, axis=-1, keepdims=True)
                outs.append((_dot(p_w.astype(BF16), vw) + _dot(p_c.astype(BF16), vc_ref[p])) / l)
            finish(p, outs)

    @pl.when(g >= n_g)
    def _():
        for p in range(H_NA // 2):
            outs = []
            for half in range(2):
                q = head_q(p, half)
                s_c = _dot_nt(q, kc_ref[p])
                p_c = jnp.exp(s_c - jnp.max(s_c, axis=-1, keepdims=True))
                outs.append(_dot(p_c.astype(BF16), vc_ref[p]) / jnp.sum(p_c, axis=-1, keepdims=True))
            finish(p, outs)


def _na_attention(q, k, v, bias, with_ctx):
    n_g = ROWS // NA_QROWS
    n_steps = n_g + (1 if with_ctx else 0)
    xq = NX // TQ
    qrow = lambda b, g: jnp.where(g < n_g, b * n_g + g, xq + b)
    kind = lambda b, g: (jnp.where(g == 0, 0, jnp.where(g >= n_g - 1, 2, 1)), 0, 0, 0)
    hp = H_NA // 2
    return pl.pallas_call(
        functools.partial(_na_kernel, n_g=n_g),
        grid=(B, n_steps),
        in_specs=[
            pl.BlockSpec((hp, TQ, LANE), lambda b, g: (0, qrow(b, g), 0)),
            pl.BlockSpec((hp, S, LANE), lambda b, g: (0, b, 0)),
            pl.BlockSpec((hp, C, LANE), lambda b, g: (0, NX // C + b, 0)),
            pl.BlockSpec((hp, S, LANE), lambda b, g: (0, b, 0)),
            pl.BlockSpec((hp, C, LANE), lambda b, g: (0, NX // C + b, 0)),
            pl.BlockSpec((1, H_NA, TQ, NA_KROWS * GRID_W), kind),
        ],
        out_specs=pl.BlockSpec((TQ, NA_W), lambda b, g: (qrow(b, g), 0)),
        out_shape=jax.ShapeDtypeStruct((N, NA_W), BF16),
        compiler_params=pltpu.CompilerParams(dimension_semantics=("parallel", "arbitrary"),
                                             vmem_limit_bytes=VMEM_BIG),
        name="na_attention",
    )(q, k, k, v, v, bias)


def _na_bias_table(rpb_l):
    qc = np.arange(GRID_W)
    kc = np.arange(GRID_W)
    ws = np.clip(qc - WIN_C // 2, 0, GRID_W - WIN_C)
    col_ok = (kc[None, :] >= ws[:, None]) & (kc[None, :] < ws[:, None] + WIN_C)
    dc = np.clip(kc[None, :] - qc[:, None] + WIN_C - 1, 0, 2 * WIN_C - 2)
    by_col = rpb_l[:, :, dc]
    n_g = ROWS // NA_QROWS
    tabs = []
    for g in (0, 1, n_g - 1):
        base = int(np.clip(g * NA_QROWS - WIN_R // 2, 0, ROWS - NA_KROWS))
        r = g * NA_QROWS + np.arange(NA_QROWS)
        rs = np.clip(r - WIN_R // 2, 0, ROWS - WIN_R)
        kr = base + np.arange(NA_KROWS)
        row_ok = (kr[None, :] >= rs[:, None]) & (kr[None, :] < rs[:, None] + WIN_R)
        ri = np.clip(kr[None, :] - r[:, None] + WIN_R - 1, 0, 2 * WIN_R - 2)
        ok = row_ok[:, None, :, None] & col_ok[None, :, None, :]
        bias = by_col[:, ri].transpose(0, 1, 3, 2, 4)
        tabs.append(jnp.where(jnp.asarray(ok)[None], bias, NEG).reshape(H_NA, TQ, NA_KROWS * GRID_W))
    return jnp.stack(tabs).astype(F32)


def _merge_core(ym_ref, yn_ref, yc_ref, sg_ref, h_ref, mod_ref, gpost_ref, gffn_ref, wbr_ref, wout_ref):
    o1, o2 = H_MLA * VD, H_MLA * VD + NA_W
    z = (sg_ref[:, 0:D].astype(F32) * _dot(ym_ref[...], wbr_ref[0:o1, :])
         + sg_ref[:, D:2 * D].astype(F32) * _dot(yn_ref[...], wbr_ref[o1:o2, :])
         + sg_ref[:, 2 * D:3 * D].astype(F32) * _dot(yc_ref[...], wbr_ref[o2:, :]))
    o = _dot(z.astype(BF16), wout_ref[...])
    h1 = h_ref[...] + mod_ref[0, 2:3, :] * _rms(o, gpost_ref[...])
    t = _rms(h1, gffn_ref[...]) * (1.0 + mod_ref[0, 4:5, :]) + mod_ref[0, 3:4, :]
    return h1, t


def _merge_kernel(ym_ref, yn_ref, yc_ref, sg_ref, h_ref, mod_ref, gpost_ref, gffn_ref, wbr_ref, wout_ref,
                  h1_ref, t_ref):
    h1, t = _merge_core(ym_ref, yn_ref, yc_ref, sg_ref, h_ref, mod_ref, gpost_ref, gffn_ref, wbr_ref, wout_ref)
    h1_ref[...] = h1
    t_ref[...] = t.astype(BF16)


def _merge_route_kernel(ym_ref, yn_ref, yc_ref, sg_ref, h_ref, mod_ref, gpost_ref, gffn_ref, wbr_ref, wout_ref,
                        wr_ref, h1_ref, t_ref, e_ref, w_ref):
    h1, t = _merge_core(ym_ref, yn_ref, yc_ref, sg_ref, h_ref, mod_ref, gpost_ref, gffn_ref, wbr_ref, wout_ref)
    h1_ref[...] = h1
    t_ref[...] = t.astype(BF16)
    logits = jnp.dot(t, wr_ref[...], preferred_element_type=F32, precision=lax.Precision.HIGHEST)
    lane = lax.broadcasted_iota(jnp.int32, logits.shape, 1)
    logits = jnp.where(lane < N_EXP, logits, -jnp.inf)
    l1 = jnp.max(logits, axis=-1, keepdims=True)
    e1 = jnp.min(jnp.where(logits == l1, lane, LANE), axis=-1, keepdims=True)
    rest = jnp.where(lane == e1, -jnp.inf, logits)
    l2 = jnp.max(rest, axis=-1, keepdims=True)
    e2 = jnp.min(jnp.where(rest == l2, lane, LANE), axis=-1, keepdims=True)
    ex = jnp.exp(l2 - l1)
    den = 1.0 + ex
    e_ref[...] = jnp.where(lane == 0, e1, jnp.where(lane == 1, e2, 0))
    w_ref[...] = jnp.where(lane == 0, 1.0 / den, jnp.where(lane == 1, ex / den, 0.0))


def _merge(ym, yn, yc, sg, h, mod_l, gpost, gffn, wl, n_rows, w_router=None):
    row = lambda i: (i, 0)
    in_specs = [
        pl.BlockSpec((TM, H_MLA * VD), row),
        pl.BlockSpec((TM, NA_W), row),
        pl.BlockSpec((TM, CM_W), row),
        pl.BlockSpec((TM, 3 * D), row),
        pl.BlockSpec((TM, D), row),
        pl.BlockSpec((1, N_MOD, D), _mod_index(S // TM)),
        _const_spec((1, D)),
        _const_spec((1, D)),
        _const_spec((D, D)),
        _const_spec((D, D)),
    ]
    out_shape = [jax.ShapeDtypeStruct((n_rows, D), F32), jax.ShapeDtypeStruct((n_rows, D), BF16)]
    out_specs = [pl.BlockSpec((TM, D), row), pl.BlockSpec((TM, D), row)]
    args = [ym, yn, yc, sg, h, mod_l, gpost, gffn, wl["w_br"], wl["w_out"]]
    if w_router is None:
        body, name = _merge_kernel, "merge"
    else:
        body, name = _merge_route_kernel, "merge_route"
        in_specs.append(_const_spec((D, LANE)))
        args.append(w_router)
        out_shape += [jax.ShapeDtypeStruct((n_rows, LANE), jnp.int32), jax.ShapeDtypeStruct((n_rows, LANE), F32)]
        out_specs += [pl.BlockSpec((TM, LANE), row), pl.BlockSpec((TM, LANE), row)]
    return pl.pallas_call(
        body,
        grid=(n_rows // TM,),
        in_specs=in_specs,
        out_specs=out_specs,
        out_shape=out_shape,
        compiler_params=pltpu.CompilerParams(dimension_semantics=("parallel",), vmem_limit_bytes=VMEM_BIG),
        name=name,
    )(*args)


def _ffn_kernel(t_ref, h1_ref, mod_ref, g_ref, w1_ref, w3_ref, w2_ref, o_ref):
    t = t_ref[...]
    half = D_FF // 2
    f = None
    for j in range(2):
        sl = slice(j * half, (j + 1) * half)
        a = _dot(t, w1_ref[:, sl])
        b = _dot(t, w3_ref[:, sl])
        part = _dot((a * jax.nn.sigmoid(a) * b).astype(BF16), w2_ref[sl, :])
        f = part if f is None else f + part
    o_ref[...] = h1_ref[...] + mod_ref[0, 5:6, :] * _rms(f, g_ref[...])


def _dense_ffn(t, h1, mod_l, g, w1, w3, w2):
    row = lambda i: (i, 0)
    return pl.pallas_call(
        _ffn_kernel,
        grid=(N // TM,),
        in_specs=[
            pl.BlockSpec((TM, D), row),
            pl.BlockSpec((TM, D), row),
            pl.BlockSpec((1, N_MOD, D), _mod_index(S // TM)),
            _const_spec((1, D)),
            _const_spec((D, D_FF)),
            _const_spec((D, D_FF)),
            _const_spec((D_FF, D)),
        ],
        out_specs=pl.BlockSpec((TM, D), row),
        out_shape=jax.ShapeDtypeStruct((N, D), F32),
        compiler_params=pltpu.CompilerParams(dimension_semantics=("parallel",), vmem_limit_bytes=VMEM_BIG),
        name="dense_ffn",
    )(t, h1, mod_l, g, w1, w3, w2)


def _moe_kernel(be_ref, nu_ref, x_ref, w1_ref, w3_ref, w2_ref, y_ref, acc_ref):
    j = pl.program_id(0)
    f = pl.program_id(1)
    used = j < nu_ref[0]

    @pl.when(used)
    def _():
        x = x_ref[...]
        a = _dot(x, w1_ref[0])
        b = _dot(x, w3_ref[0])
        part = _dot((a * jax.nn.sigmoid(a) * b).astype(BF16), w2_ref[0])

        @pl.when(f == 0)
        def _():
            acc_ref[...] = part

        @pl.when(f > 0)
        def _():
            acc_ref[...] += part

    @pl.when(f == D_FFE // FT - 1)
    def _():
        y_ref[...] = jnp.where(used, acc_ref[...], 0.0)


def _moe_experts(x_pad, blk_expert, n_used, w1, w3, w2):
    grid_spec = pltpu.PrefetchScalarGridSpec(
        num_scalar_prefetch=2,
        grid=(N_BLK, D_FFE // FT),
        in_specs=[
            pl.BlockSpec((MOE_BLK, D), lambda j, f, be, nu: (j, 0)),
            pl.BlockSpec((1, D, FT), lambda j, f, be, nu: (be[j], 0, f)),
            pl.BlockSpec((1, D, FT), lambda j, f, be, nu: (be[j], 0, f)),
            pl.BlockSpec((1, FT, D), lambda j, f, be, nu: (be[j], f, 0)),
        ],
        out_specs=pl.BlockSpec((MOE_BLK, D), lambda j, f, be, nu: (j, 0)),
        scratch_shapes=[pltpu.VMEM((MOE_BLK, D), F32)],
    )
    return pl.pallas_call(
        _moe_kernel,
        grid_spec=grid_spec,
        out_shape=jax.ShapeDtypeStruct((N_PAD, D), F32),
        compiler_params=pltpu.CompilerParams(dimension_semantics=("parallel", "arbitrary"),
                                             vmem_limit_bytes=VMEM_BIG),
        name="moe_experts",
    )(blk_expert, n_used, x_pad, w1, w3, w2)


def _combine_kernel(ya_ref, yb_ref, w_ref, h1_ref, mod_ref, g_ref, o_ref):
    f = w_ref[:, 0:1] * ya_ref[...] + w_ref[:, 1:2] * yb_ref[...]
    o_ref[...] = h1_ref[...] + mod_ref[0, 5:6, :] * _rms(f, g_ref[...])


def _moe_combine(ya, yb, w, h1, mod_l, g):
    row = lambda i: (i, 0)
    return pl.pallas_call(
        _combine_kernel,
        grid=(NX // TM,),
        in_specs=[
            pl.BlockSpec((TM, D), row),
            pl.BlockSpec((TM, D), row),
            pl.BlockSpec((TM, LANE), row),
            pl.BlockSpec((TM, D), row),
            pl.BlockSpec((1, N_MOD, D), _mod_index(S // TM)),
            _const_spec((1, D)),
        ],
        out_specs=pl.BlockSpec((TM, D), row),
        out_shape=jax.ShapeDtypeStruct((NX, D), F32),
        compiler_params=pltpu.CompilerParams(dimension_semantics=("parallel",), vmem_limit_bytes=VMEM_BIG),
        name="moe_combine",
    )(ya, yb, w, h1, mod_l, g)


def _moe_layout(e):
    e_flat = e.reshape(N_ASG)
    onehot = (e_flat[:, None] == jnp.arange(N_EXP, dtype=jnp.int32)[None, :]).astype(jnp.int32)
    csum = jnp.cumsum(onehot, axis=0)
    rank = jnp.sum(csum * onehot, axis=1) - 1
    counts = csum[-1]
    padded = (counts + MOE_BLK - 1) // MOE_BLK * MOE_BLK
    pad_end = jnp.cumsum(padded)
    pad_start = pad_end - padded
    dest = jnp.sum(pad_start[None, :] * onehot, axis=1) + rank
    blk_start = jnp.arange(N_BLK, dtype=jnp.int32) * MOE_BLK
    blk_expert = jnp.minimum(jnp.sum((blk_start[:, None] >= pad_end[None, :]).astype(jnp.int32), axis=1), N_EXP - 1)
    n_used = (pad_end[-1] // MOE_BLK).astype(jnp.int32).reshape(1)
    return dest.astype(jnp.int32), blk_expert.astype(jnp.int32), n_used


def _rope_rot(w):
    half, quarter = ROPE // 2, ROPE // 4
    return jnp.concatenate([-w[..., quarter:half], w[..., :quarter], -w[..., half + quarter:], w[..., half:half + quarter]],
                           axis=-1)


def _rope_tables():
    half = ROPE // 2
    inv = ROPE_BASE ** (-(jnp.arange(0, half, 2, dtype=F32) / half))
    t = jnp.arange(S)
    ang_r = (t // GRID_W).astype(F32)[:, None] * inv
    ang_c = (t % GRID_W).astype(F32)[:, None] * inv
    ang = jnp.concatenate([ang_r, ang_r, ang_c, ang_c], axis=-1)
    cos32, sin32 = jnp.cos(ang), jnp.sin(ang)
    cos = jnp.ones((S + TM, LANE), F32).at[:S, NOPE:NOPE + ROPE].set(cos32)
    sin = jnp.zeros((S + TM, LANE), F32).at[:S, NOPE:NOPE + ROPE].set(sin32)
    return cos, sin


def _pack_layer(l, w_in, g_q, w_uq, g_kv, w_ukv, g_v, b_v, w_sp, b_sp, w_br, w_out):
    wi = w_in[l]
    offs = np.cumsum((0, Q_LORA, KV_LORA, ROPE, NA_W, NA_W, NA_W, CM_W, CM_W))
    cq, ckv, kr, naq, nak, nav, u, v = (wi[:, offs[i]:offs[i + 1]] for i in range(8))
    gl = wi[:, offs[8]:]
    zl = jnp.zeros((D, NOPE), F32)
    zr = jnp.zeros((D, LANE - NOPE - ROPE), F32)
    w_in_p = jnp.concatenate([cq, ckv, zl, kr, zr, zl, _rope_rot(kr), zr, naq, nak, nav, u, v, gl], axis=1).astype(BF16)
    uq = w_uq[l].reshape(Q_LORA, H_MLA, QK)
    zq = jnp.zeros((Q_LORA, H_MLA, LANE - QK), F32)
    w_qa = jnp.concatenate([uq, zq], axis=2).reshape(Q_LORA, H_MLA * LANE).astype(BF16)
    w_qb = jnp.concatenate([jnp.zeros((Q_LORA, H_MLA, NOPE), F32), _rope_rot(uq[:, :, NOPE:]), zq], axis=2)
    w_qb = w_qb.reshape(Q_LORA, H_MLA * LANE).astype(BF16)
    ukv = w_ukv[l].reshape(KV_LORA, H_MLA, NOPE + VD)
    w_uk = jnp.concatenate([ukv[:, :, :NOPE], jnp.zeros((KV_LORA, H_MLA, LANE - NOPE), F32)], axis=2)
    w_uk = w_uk.reshape(KV_LORA, H_MLA * LANE).astype(BF16)
    w_uv = ukv[:, :, NOPE:].reshape(KV_LORA, H_MLA * VD).astype(BF16)
    b_sp_t = jnp.repeat(b_sp[l].T, CM_W // CM_G, axis=1)
    return dict(
        w_in=w_in_p, g_q=g_q[l].reshape(1, -1), w_qa=w_qa, w_qb=w_qb, g_kv=g_kv[l].reshape(1, -1), w_uk=w_uk,
        w_uv=w_uv, g_v=g_v[l].reshape(1, -1), b_v=b_v[l].reshape(1, -1), w_sp=w_sp[l].astype(BF16), b_sp=b_sp_t,
        w_br=w_br[l].astype(BF16), w_out=w_out[l].astype(BF16))


def kernel(x, c, ctx, c_ctx, w_mod, b_mod, g_pre_mix, g_post_mix, g_pre_ffn, g_post_ffn, w_in, g_q, w_uq, g_kv, w_ukv, rpb, g_v, b_v, w_sp, b_sp, w_br, w_out, w_ffn1, w_ffn3, w_ffn2, w_router, w_moe1, w_moe3, w_moe2):
    h = jnp.concatenate([x.reshape(NX, D), ctx.reshape(NC, D)], axis=0)
    c_all = jnp.concatenate([c, c_ctx[None, :], jnp.zeros((16 - B - 1, D), F32)], axis=0)
    mod = _modulation(c_all, w_mod, b_mod).reshape(DEPTH, 16, N_MOD, D)
    rope_cos, rope_sin = _rope_tables()
    row1 = lambda a: a.reshape(1, -1)

    for l in range(DEPTH):
        last = l == DEPTH - 1
        wl = _pack_layer(l, w_in, g_q, w_uq, g_kv, w_ukv, g_v, b_v, w_sp, b_sp, w_br, w_out)
        mod_l = mod[l]
        q, k, v, naq, nak, nav, ycm, sg = _inproj(h, mod_l, row1(g_pre_mix[l]), wl, rope_cos, rope_sin)
        ym = _mla_attention(q, k, v, with_ctx=not last)
        yn = _na_attention(naq, nak, nav, _na_bias_table(rpb[l]), with_ctx=not last)
        if l % 2 == 0:
            i = l // 2
            h1, t = _merge(ym, yn, ycm, sg, h, mod_l, row1(g_post_mix[l]), row1(g_pre_ffn[l]), wl, N)
            h = _dense_ffn(t, h1, mod_l, row1(g_post_ffn[l]), w_ffn1[i].astype(BF16), w_ffn3[i].astype(BF16),
                           w_ffn2[i].astype(BF16))
        else:
            i = l // 2
            wr = jnp.concatenate([w_router[i], jnp.zeros((D, LANE - N_EXP), F32)], axis=1)
            h1, t, e, w = _merge(ym, yn, ycm, sg, h, mod_l, row1(g_post_mix[l]), row1(g_pre_ffn[l]), wl, NX, wr)
            dest, blk_expert, n_used = _moe_layout(e[:, :2])
            slot_tok = jnp.zeros((N_PAD,), jnp.int32).at[dest].set(jnp.arange(N_ASG, dtype=jnp.int32) // 2)
            x_pad = jnp.take(t, slot_tok, axis=0)
            y = _moe_experts(x_pad, blk_expert, n_used, w_moe1[i].astype(BF16), w_moe3[i].astype(BF16),
                             w_moe2[i].astype(BF16))
            d2 = dest.reshape(NX, 2)
            h = _moe_combine(jnp.take(y, d2[:, 0], axis=0), jnp.take(y, d2[:, 1], axis=0), w, h1, mod_l,
                             row1(g_post_ffn[l]))
    return h[:NX].reshape(B, S, D)
```

```python
import functools

import numpy as np
import jax
import jax.numpy as jnp
from jax import lax
from jax.experimental import pallas as pl
from jax.experimental.pallas import tpu as pltpu

F32 = jnp.float32
BF16 = jnp.bfloat16

D = 1024
B = 8
S = 4096
DEPTH = 2
GRID_W = 64
ROWS = S // GRID_W
C = 256
EPS = 1e-6
NEG = -1e30
LOG2E = 1.4426950408889634
N_MOD = 6

H_MLA = 8
Q_LORA = 384
KV_LORA = 256
NOPE = 64
ROPE = 32
VD = 64
QK = NOPE + ROPE
ROPE_BASE = 10000.0

H_NA = 4
HD_NA = 64
NA_W = H_NA * HD_NA
WIN_R = 8
WIN_C = 16
NA_QROWS = 4
NA_KROWS = 12

CM_G = 4
CM_CHUNK = 128
CM_W = 256

D_FF = 2816
N_EXP = 8
D_FFE = 3584
MOE_BLK = 512

NX = B * S
NC = B * C
N = NX + NC
LANE = 128

TM = 512
TQ = 256
TK = 512
FT = 512
N_ASG = NX * 2
N_BLK = N_ASG // MOE_BLK + N_EXP
N_PAD = N_BLK * MOE_BLK

O_CQ, O_CKV, O_KRA, O_KRB, O_NAQ, O_NAK, O_NAV, O_U, O_V, O_GL = 0, 384, 640, 768, 896, 1152, 1408, 1664, 1920, 2176
W_IN_PACKED = O_GL + 3 * D

VMEM_BIG = 56 * 1024 * 1024


def _rms(xf, g):
    return xf * lax.rsqrt(jnp.mean(xf * xf, axis=-1, keepdims=True) + EPS) * g


def _dot(a, b):
    return jnp.dot(a, b, preferred_element_type=F32)


def _dot_nt(a, b):
    return lax.dot_general(a, b, (((1,), (1,)), ((), ())), preferred_element_type=F32)


def _const_spec(shape):
    nd = len(shape)
    return pl.BlockSpec(shape, lambda *_: (0,) * nd, pipeline_mode=pl.Buffered(1))


def _mod_index(tiles_per_batch):
    return lambda i: (jnp.minimum(i // tiles_per_batch, B), 0, 0)


def _mod_kernel(c_ref, w_ref, b_ref, o_ref):
    c = c_ref[...]
    sc = c * jax.nn.sigmoid(c)
    o_ref[0] = jnp.dot(sc, w_ref[0], preferred_element_type=F32, precision=lax.Precision.HIGHEST) + b_ref[0]


def _modulation(c_all, w_mod, b_mod):
    tn = 1536
    return pl.pallas_call(
        _mod_kernel,
        grid=(DEPTH, N_MOD * D // tn),
        in_specs=[
            pl.BlockSpec((16, D), lambda l, j: (0, 0)),
            pl.BlockSpec((1, D, tn), lambda l, j: (l, 0, j)),
            pl.BlockSpec((1, 1, tn), lambda l, j: (l, 0, j)),
        ],
        out_specs=pl.BlockSpec((1, 16, tn), lambda l, j: (l, 0, j)),
        out_shape=jax.ShapeDtypeStruct((DEPTH, 16, N_MOD * D), F32),
        compiler_params=pltpu.CompilerParams(dimension_semantics=("parallel", "parallel"), vmem_limit_bytes=VMEM_BIG),
        name="modulation",
    )(c_all, w_mod, b_mod.reshape(DEPTH, 1, N_MOD * D))


def _inproj_kernel(h_ref, mod_ref, gpre_ref, win_ref, gq_ref, wqa_ref, wqb_ref, gkv_ref, wuk_ref, wuv_ref,
                   cos_ref, sin_ref, gv_ref, bv_ref, wsp_ref, bsp_ref,
                   q_ref, k_ref, v_ref, naq_ref, nak_ref, nav_ref, ycm_ref, sg_ref):
    n = _rms(h_ref[...], gpre_ref[...]) * (1.0 + mod_ref[0, 1:2, :]) + mod_ref[0, 0:1, :]
    nb = n.astype(BF16)

    def proj(a, b):
        return _dot(nb, win_ref[:, a:b])

    cos = cos_ref[...]
    sin = sin_ref[...]

    cqn = _rms(proj(O_CQ, O_CKV), gq_ref[...]).astype(BF16)
    qa = _dot(cqn, wqa_ref[...])
    qb = _dot(cqn, wqb_ref[...])
    for hh in range(H_MLA):
        sl = slice(hh * LANE, (hh + 1) * LANE)
        q_ref[hh] = ((qa[:, sl] * cos + qb[:, sl] * sin) * (QK ** -0.5 * LOG2E)).astype(BF16)
    ckvn = _rms(proj(O_CKV, O_KRA), gkv_ref[...]).astype(BF16)
    krp = proj(O_KRA, O_KRB) * cos + proj(O_KRB, O_NAQ) * sin
    kk = _dot(ckvn, wuk_ref[...])
    for hh in range(H_MLA):
        k_ref[hh] = (kk[:, hh * LANE:(hh + 1) * LANE] + krp).astype(BF16)
    vv = _dot(ckvn, wuv_ref[...])
    for p in range(H_MLA // 2):
        v_ref[p] = vv[:, p * LANE:(p + 1) * LANE].astype(BF16)

    naq = proj(O_NAQ, O_NAK) * (HD_NA ** -0.5)
    nak = proj(O_NAK, O_NAV)
    nav = proj(O_NAV, O_U)
    for p in range(H_NA // 2):
        sl = slice(p * LANE, (p + 1) * LANE)
        naq_ref[p] = naq[:, sl].astype(BF16)
        nak_ref[p] = nak[:, sl].astype(BF16)
        nav_ref[p] = nav[:, sl].astype(BF16)

    u = proj(O_U, O_V)
    v = proj(O_V, O_GL)
    mu = jnp.mean(v, axis=-1, keepdims=True)
    vc = v - mu
    vn = (vc * lax.rsqrt(jnp.mean(vc * vc, axis=-1, keepdims=True) + EPS) * gv_ref[...] + bv_ref[...]).astype(BF16)
    grp = lax.broadcasted_iota(jnp.int32, (CM_CHUNK, CM_W), 1) // (CM_W // CM_G)
    for ch in range(TM // CM_CHUNK):
        rs = slice(ch * CM_CHUNK, (ch + 1) * CM_CHUNK)
        vch = vn[rs, :]
        sp = _dot(wsp_ref[CM_G - 1], vch)
        for g in range(CM_G - 2, -1, -1):
            sp = jnp.where(grp == g, _dot(wsp_ref[g], vch), sp)
        ycm_ref[rs, :] = (u[rs, :] * (sp + bsp_ref[...])).astype(BF16)

    for j in range(3):
        sg_ref[:, j * D:(j + 1) * D] = jax.nn.sigmoid(proj(O_GL + j * D, O_GL + (j + 1) * D)).astype(BF16)


def _inproj(h, mod_l, gpre, wl, rope_cos, rope_sin):
    tiles_x = S // TM
    n_tiles = N // TM
    pos_idx = lambda i: (jnp.where(i < NX // TM, i % tiles_x, tiles_x), 0)
    row = lambda i: (i, 0)
    hrow = lambda i: (0, i, 0)
    in_specs = [
        pl.BlockSpec((TM, D), row),
        pl.BlockSpec((1, N_MOD, D), _mod_index(tiles_x)),
        _const_spec((1, D)),
        _const_spec((D, W_IN_PACKED)),
        _const_spec((1, Q_LORA)),
        _const_spec((Q_LORA, H_MLA * LANE)),
        _const_spec((Q_LORA, H_MLA * LANE)),
        _const_spec((1, KV_LORA)),
        _const_spec((KV_LORA, H_MLA * LANE)),
        _const_spec((KV_LORA, H_MLA * VD)),
        pl.BlockSpec((TM, LANE), pos_idx),
        pl.BlockSpec((TM, LANE), pos_idx),
        _const_spec((1, CM_W)),
        _const_spec((1, CM_W)),
        _const_spec((CM_G, CM_CHUNK, CM_CHUNK)),
        _const_spec((CM_CHUNK, CM_W)),
    ]
    out_shape = [
        jax.ShapeDtypeStruct((H_MLA, N, LANE), BF16),
        jax.ShapeDtypeStruct((H_MLA, N, LANE), BF16),
        jax.ShapeDtypeStruct((H_MLA // 2, N, LANE), BF16),
        jax.ShapeDtypeStruct((H_NA // 2, N, LANE), BF16),
        jax.ShapeDtypeStruct((H_NA // 2, N, LANE), BF16),
        jax.ShapeDtypeStruct((H_NA // 2, N, LANE), BF16),
        jax.ShapeDtypeStruct((N, CM_W), BF16),
        jax.ShapeDtypeStruct((N, 3 * D), BF16),
    ]
    out_specs = [
        pl.BlockSpec((H_MLA, TM, LANE), hrow),
        pl.BlockSpec((H_MLA, TM, LANE), hrow),
        pl.BlockSpec((H_MLA // 2, TM, LANE), hrow),
        pl.BlockSpec((H_NA // 2, TM, LANE), hrow),
        pl.BlockSpec((H_NA // 2, TM, LANE), hrow),
        pl.BlockSpec((H_NA // 2, TM, LANE), hrow),
        pl.BlockSpec((TM, CM_W), row),
        pl.BlockSpec((TM, 3 * D), row),
    ]
    return pl.pallas_call(
        _inproj_kernel,
        grid=(n_tiles,),
        in_specs=in_specs,
        out_specs=out_specs,
        out_shape=out_shape,
        compiler_params=pltpu.CompilerParams(dimension_semantics=("parallel",), vmem_limit_bytes=VMEM_BIG),
        name="inproj",
    )(h, mod_l, gpre, wl["w_in"], wl["g_q"], wl["w_qa"], wl["w_qb"], wl["g_kv"], wl["w_uk"], wl["w_uv"],
      rope_cos, rope_sin, wl["g_v"], wl["b_v"], wl["w_sp"], wl["b_sp"])


def _mla_head(q, kc, vc, kx, vx):
    s_c = _dot_nt(q, kc)
    m = jnp.max(s_c, axis=-1, keepdims=True)
    if kx is not None:
        s_x = _dot_nt(q, kx)
        m = jnp.maximum(m, jnp.max(s_x, axis=-1, keepdims=True))
    p_c = jnp.exp2(s_c - m)
    l = jnp.sum(p_c, axis=-1, keepdims=True)
    acc = _dot(p_c.astype(BF16), vc)
    if kx is not None:
        p_x = jnp.exp2(s_x - m)
        l = l + jnp.sum(p_x, axis=-1, keepdims=True)
        acc = acc + _dot(p_x.astype(BF16), vx)
    return acc / l


def _mla_kernel(q_ref, kx_ref, kc_ref, vx_ref, vc_ref, o_ref, *, n_qx, with_ctx):
    lane = lax.broadcasted_iota(jnp.int32, (TQ, LANE), 1)

    def step(with_x):
        outs = [_mla_head(q_ref[hh], kc_ref[hh], vc_ref[0], kx_ref[hh] if with_x else None,
                          vx_ref[0] if with_x else None) for hh in range(2)]
        o_ref[...] = jnp.where(lane < VD, outs[0], outs[1]).astype(o_ref.dtype)

    if not with_ctx:
        step(True)
        return
    is_x = pl.program_id(2) < n_qx
    pl.when(is_x)(lambda: step(True))
    pl.when(jnp.logical_not(is_x))(lambda: step(False))


def _mla_attention(q, k, v, with_ctx):
    n_qx = S // TQ
    n_q = n_qx + (1 if with_ctx else 0)
    xq = NX // TQ
    qrow = lambda b, q_: jnp.where(q_ < n_qx, b * n_qx + q_, xq + b)
    return pl.pallas_call(
        functools.partial(_mla_kernel, n_qx=n_qx, with_ctx=with_ctx),
        grid=(B, H_MLA // 2, n_q),
        in_specs=[
            pl.BlockSpec((2, TQ, LANE), lambda b, p, q_: (p, qrow(b, q_), 0)),
            pl.BlockSpec((2, S, LANE), lambda b, p, q_: (p, b, 0)),
            pl.BlockSpec((2, C, LANE), lambda b, p, q_: (p, NX // C + b, 0)),
            pl.BlockSpec((1, S, LANE), lambda b, p, q_: (p, b, 0)),
            pl.BlockSpec((1, C, LANE), lambda b, p, q_: (p, NX // C + b, 0)),
        ],
        out_specs=pl.BlockSpec((TQ, LANE), lambda b, p, q_: (qrow(b, q_), p)),
        out_shape=jax.ShapeDtypeStruct((N, H_MLA * VD), BF16),
        compiler_params=pltpu.CompilerParams(dimension_semantics=("parallel", "parallel", "arbitrary"),
                                             vmem_limit_bytes=VMEM_BIG),
        name="mla_attention",
    )(q, k, k, v, v)


def _na_kernel(q_ref, kx_ref, kc_ref, vx_ref, vc_ref, bias_ref, o_ref, *, n_g):
    g = pl.program_id(1)
    lane = lax.broadcasted_iota(jnp.int32, (TQ, LANE), 1)

    def head_q(p, half):
        qp = q_ref[p]
        return jnp.where((lane < HD_NA) == (half == 0), qp, jnp.zeros_like(qp))

    def finish(p, outs):
        o_ref[:, p * LANE:(p + 1) * LANE] = jnp.where(lane < HD_NA, outs[0], outs[1]).astype(o_ref.dtype)

    @pl.when(g < n_g)
    def _():
        base = jnp.clip(g * NA_QROWS - WIN_R // 2, 0, ROWS - NA_KROWS)
        ks = pl.ds(pl.multiple_of(base * GRID_W, GRID_W), NA_KROWS * GRID_W)
        for p in range(H_NA // 2):
            kw = kx_ref[p, ks, :]
            vw = vx_ref[p, ks, :]
            outs = []
            for half in range(2):
                q = head_q(p, half)
                s_w = _dot_nt(q, kw) + bias_ref[0, 2 * p + half]
                s_c = _dot_nt(q, kc_ref[p])
                m = jnp.maximum(jnp.max(s_w, axis=-1, keepdims=True), jnp.max(s_c, axis=-1, keepdims=True))
                p_w = jnp.exp(s_w - m)
                p_c = jnp.exp(s_c - m)
                l = jnp.sum(p_w, axis=-1, keepdims=True) + jnp.sum(p_c, axis=-1, keepdims=True)
                outs.append((_dot(p_w.astype(BF16), vw) + _dot(p_c.astype(BF16), vc_ref[p])) / l)
            finish(p, outs)

    @pl.when(g >= n_g)
    def _():
        for p in range(H_NA // 2):
            outs = []
            for half in range(2):
                q = head_q(p, half)
                s_c = _dot_nt(q, kc_ref[p])
                p_c = jnp.exp(s_c - jnp.max(s_c, axis=-1, keepdims=True))
                outs.append(_dot(p_c.astype(BF16), vc_ref[p]) / jnp.sum(p_c, axis=-1, keepdims=True))
            finish(p, outs)


def _na_attention(q, k, v, bias, with_ctx):
    n_g = ROWS // NA_QROWS
    n_steps = n_g + (1 if with_ctx else 0)
    xq = NX // TQ
    qrow = lambda b, g: jnp.where(g < n_g, b * n_g + g, xq + b)
    kind = lambda b, g: (jnp.where(g == 0, 0, jnp.where(g >= n_g - 1, 2, 1)), 0, 0, 0)
    hp = H_NA // 2
    return pl.pallas_call(
        functools.partial(_na_kernel, n_g=n_g),
        grid=(B, n_steps),
        in_specs=[
            pl.BlockSpec((hp, TQ, LANE), lambda b, g: (0, qrow(b, g), 0)),
            pl.BlockSpec((hp, S, LANE), lambda b, g: (0, b, 0)),
            pl.BlockSpec((hp, C, LANE), lambda b, g: (0, NX // C + b, 0)),
            pl.BlockSpec((hp, S, LANE), lambda b, g: (0, b, 0)),
            pl.BlockSpec((hp, C, LANE), lambda b, g: (0, NX // C + b, 0)),
            pl.BlockSpec((1, H_NA, TQ, NA_KROWS * GRID_W), kind),
        ],
        out_specs=pl.BlockSpec((TQ, NA_W), lambda b, g: (qrow(b, g), 0)),
        out_shape=jax.ShapeDtypeStruct((N, NA_W), BF16),
        compiler_params=pltpu.CompilerParams(dimension_semantics=("parallel", "arbitrary"),
                                             vmem_limit_bytes=VMEM_BIG),
        name="na_attention",
    )(q, k, k, v, v, bias)


def _na_bias_table(rpb_l):
    qc = np.arange(GRID_W)
    kc = np.arange(GRID_W)
    ws = np.clip(qc - WIN_C // 2, 0, GRID_W - WIN_C)
    col_ok = (kc[None, :] >= ws[:, None]) & (kc[None, :] < ws[:, None] + WIN_C)
    dc = np.clip(kc[None, :] - qc[:, None] + WIN_C - 1, 0, 2 * WIN_C - 2)
    by_col = rpb_l[:, :, dc]
    n_g = ROWS // NA_QROWS
    tabs = []
    for g in (0, 1, n_g - 1):
        base = int(np.clip(g * NA_QROWS - WIN_R // 2, 0, ROWS - NA_KROWS))
        r = g * NA_QROWS + np.arange(NA_QROWS)
        rs = np.clip(r - WIN_R // 2, 0, ROWS - WIN_R)
        kr = base + np.arange(NA_KROWS)
        row_ok = (kr[None, :] >= rs[:, None]) & (kr[None, :] < rs[:, None] + WIN_R)
        ri = np.clip(kr[None, :] - r[:, None] + WIN_R - 1, 0, 2 * WIN_R - 2)
        ok = row_ok[:, None, :, None] & col_ok[None, :, None, :]
        bias = by_col[:, ri].transpose(0, 1, 3, 2, 4)
        tabs.append(jnp.where(jnp.asarray(ok)[None], bias, NEG).reshape(H_NA, TQ, NA_KROWS * GRID_W))
    return jnp.stack(tabs).astype(F32)


def _merge_core(ym_ref, yn_ref, yc_ref, sg_ref, h_ref, mod_ref, gpost_ref, gffn_ref, wbr_ref, wout_ref):
    o1, o2 = H_MLA * VD, H_MLA * VD + NA_W
    z = (sg_ref[:, 0:D].astype(F32) * _dot(ym_ref[...], wbr_ref[0:o1, :])
         + sg_ref[:, D:2 * D].astype(F32) * _dot(yn_ref[...], wbr_ref[o1:o2, :])
         + sg_ref[:, 2 * D:3 * D].astype(F32) * _dot(yc_ref[...], wbr_ref[o2:, :]))
    o = _dot(z.astype(BF16), wout_ref[...])
    h1 = h_ref[...] + mod_ref[0, 2:3, :] * _rms(o, gpost_ref[...])
    t = _rms(h1, gffn_ref[...]) * (1.0 + mod_ref[0, 4:5, :]) + mod_ref[0, 3:4, :]
    return h1, t


def _merge_kernel(ym_ref, yn_ref, yc_ref, sg_ref, h_ref, mod_ref, gpost_ref, gffn_ref, wbr_ref, wout_ref,
                  h1_ref, t_ref):
    h1, t = _merge_core(ym_ref, yn_ref, yc_ref, sg_ref, h_ref, mod_ref, gpost_ref, gffn_ref, wbr_ref, wout_ref)
    h1_ref[...] = h1
    t_ref[...] = t.astype(BF16)


def _merge_route_kernel(ym_ref, yn_ref, yc_ref, sg_ref, h_ref, mod_ref, gpost_ref, gffn_ref, wbr_ref, wout_ref,
                        wr_ref, h1_ref, t_ref, e_ref, w_ref):
    h1, t = _merge_core(ym_ref, yn_ref, yc_ref, sg_ref, h_ref, mod_ref, gpost_ref, gffn_ref, wbr_ref, wout_ref)
    h1_ref[...] = h1
    t_hi = t.astype(BF16)
    t_ref[...] = t_hi
    t_lo = (t - t_hi.astype(F32)).astype(BF16)
    hi = _dot(t_hi, wr_ref[...])
    logits = hi[:, :LANE] + hi[:, LANE:] + _dot(t_lo, wr_ref[:, :LANE])
    lane = lax.broadcasted_iota(jnp.int32, logits.shape, 1)
    logits = jnp.where(lane < N_EXP, logits, -jnp.inf)
    l1 = jnp.max(logits, axis=-1, keepdims=True)
    e1 = jnp.min(jnp.where(logits == l1, lane, LANE), axis=-1, keepdims=True)
    rest = jnp.where(lane == e1, -jnp.inf, logits)
    l2 = jnp.max(rest, axis=-1, keepdims=True)
    e2 = jnp.min(jnp.where(rest == l2, lane, LANE), axis=-1, keepdims=True)
    ex = jnp.exp(l2 - l1)
    den = 1.0 + ex
    e_ref[...] = jnp.where(lane == 0, e1, jnp.where(lane == 1, e2, 0))
    w_ref[...] = jnp.where(lane == 0, 1.0 / den, jnp.where(lane == 1, ex / den, 0.0))


def _merge(ym, yn, yc, sg, h, mod_l, gpost, gffn, wl, n_rows, w_router=None):
    row = lambda i: (i, 0)
    in_specs = [
        pl.BlockSpec((TM, H_MLA * VD), row),
        pl.BlockSpec((TM, NA_W), row),
        pl.BlockSpec((TM, CM_W), row),
        pl.BlockSpec((TM, 3 * D), row),
        pl.BlockSpec((TM, D), row),
        pl.BlockSpec((1, N_MOD, D), _mod_index(S // TM)),
        _const_spec((1, D)),
        _const_spec((1, D)),
        _const_spec((D, D)),
        _const_spec((D, D)),
    ]
    out_shape = [jax.ShapeDtypeStruct((n_rows, D), F32), jax.ShapeDtypeStruct((n_rows, D), BF16)]
    out_specs = [pl.BlockSpec((TM, D), row), pl.BlockSpec((TM, D), row)]
    args = [ym, yn, yc, sg, h, mod_l, gpost, gffn, wl["w_br"], wl["w_out"]]
    if w_router is None:
        body, name = _merge_kernel, "merge"
    else:
        body, name = _merge_route_kernel, "merge_route"
        in_specs.append(_const_spec((D, 2 * LANE)))
        args.append(w_router)
        out_shape += [jax.ShapeDtypeStruct((n_rows, LANE), jnp.int32), jax.ShapeDtypeStruct((n_rows, LANE), F32)]
        out_specs += [pl.BlockSpec((TM, LANE), row), pl.BlockSpec((TM, LANE), row)]
    return pl.pallas_call(
        body,
        grid=(n_rows // TM,),
        in_specs=in_specs,
        out_specs=out_specs,
        out_shape=out_shape,
        compiler_params=pltpu.CompilerParams(dimension_semantics=("parallel",), vmem_limit_bytes=VMEM_BIG),
        name=name,
    )(*args)


def _ffn_kernel(t_ref, h1_ref, mod_ref, g_ref, w1_ref, w3_ref, w2_ref, o_ref):
    t = t_ref[...]
    half = D_FF // 2
    f = None
    for j in range(2):
        sl = slice(j * half, (j + 1) * half)
        a = _dot(t, w1_ref[:, sl])
        b = _dot(t, w3_ref[:, sl])
        part = _dot((a * jax.nn.sigmoid(a) * b).astype(BF16), w2_ref[sl, :])
        f = part if f is None else f + part
    o_ref[...] = h1_ref[...] + mod_ref[0, 5:6, :] * _rms(f, g_ref[...])


def _dense_ffn(t, h1, mod_l, g, w1, w3, w2):
    row = lambda i: (i, 0)
    return pl.pallas_call(
        _ffn_kernel,
        grid=(N // TM,),
        in_specs=[
            pl.BlockSpec((TM, D), row),
            pl.BlockSpec((TM, D), row),
            pl.BlockSpec((1, N_MOD, D), _mod_index(S // TM)),
            _const_spec((1, D)),
            _const_spec((D, D_FF)),
            _const_spec((D, D_FF)),
            _const_spec((D_FF, D)),
        ],
        out_specs=pl.BlockSpec((TM, D), row),
        out_shape=jax.ShapeDtypeStruct((N, D), F32),
        compiler_params=pltpu.CompilerParams(dimension_semantics=("parallel",), vmem_limit_bytes=VMEM_BIG),
        name="dense_ffn",
    )(t, h1, mod_l, g, w1, w3, w2)


def _moe_kernel(be_ref, nu_ref, x_ref, w1_ref, w3_ref, w2_ref, y_ref):
    used = pl.program_id(0) < nu_ref[0]

    @pl.when(used)
    def _():
        x = x_ref[...]
        for f in range(D_FFE // FT):
            sl = slice(f * FT, (f + 1) * FT)
            a = _dot(x, w1_ref[0, :, sl])
            b = _dot(x, w3_ref[0, :, sl])
            part = _dot((a * jax.nn.sigmoid(a) * b).astype(BF16), w2_ref[0, sl, :])
            if f == 0:
                y_ref[...] = part
            else:
                y_ref[...] += part

    @pl.when(jnp.logical_not(used))
    def _():
        y_ref[...] = jnp.zeros(y_ref.shape, y_ref.dtype)


def _moe_experts(x_pad, blk_expert, n_used, w1, w3, w2):
    wspec = lambda shape: pl.BlockSpec(shape, lambda j, be, nu: (be[j], 0, 0), pipeline_mode=pl.Buffered(1))
    grid_spec = pltpu.PrefetchScalarGridSpec(
        num_scalar_prefetch=2,
        grid=(N_BLK,),
        in_specs=[
            pl.BlockSpec((MOE_BLK, D), lambda j, be, nu: (j, 0)),
            wspec((1, D, D_FFE)),
            wspec((1, D, D_FFE)),
            wspec((1, D_FFE, D)),
        ],
        out_specs=pl.BlockSpec((MOE_BLK, D), lambda j, be, nu: (j, 0)),
    )
    return pl.pallas_call(
        _moe_kernel,
        grid_spec=grid_spec,
        out_shape=jax.ShapeDtypeStruct((N_PAD, D), F32),
        compiler_params=pltpu.CompilerParams(dimension_semantics=("arbitrary",), vmem_limit_bytes=VMEM_BIG),
        name="moe_experts",
    )(blk_expert, n_used, x_pad, w1, w3, w2)


def _combine_kernel(ya_ref, yb_ref, w_ref, h1_ref, mod_ref, g_ref, o_ref):
    f = w_ref[:, 0:1] * ya_ref[...] + w_ref[:, 1:2] * yb_ref[...]
    o_ref[...] = h1_ref[...] + mod_ref[0, 5:6, :] * _rms(f, g_ref[...])


def _moe_combine(ya, yb, w, h1, mod_l, g):
    row = lambda i: (i, 0)
    return pl.pallas_call(
        _combine_kernel,
        grid=(NX // TM,),
        in_specs=[
            pl.BlockSpec((TM, D), row),
            pl.BlockSpec((TM, D), row),
            pl.BlockSpec((TM, LANE), row),
            pl.BlockSpec((TM, D), row),
            pl.BlockSpec((1, N_MOD, D), _mod_index(S // TM)),
            _const_spec((1, D)),
        ],
        out_specs=pl.BlockSpec((TM, D), row),
        out_shape=jax.ShapeDtypeStruct((NX, D), F32),
        compiler_params=pltpu.CompilerParams(dimension_semantics=("parallel",), vmem_limit_bytes=VMEM_BIG),
        name="moe_combine",
    )(ya, yb, w, h1, mod_l, g)


def _moe_layout(e):
    e_flat = e.reshape(N_ASG)
    onehot = (e_flat[:, None] == jnp.arange(N_EXP, dtype=jnp.int32)[None, :]).astype(jnp.int32)
    csum = jnp.cumsum(onehot, axis=0)
    rank = jnp.sum(csum * onehot, axis=1) - 1
    counts = csum[-1]
    padded = (counts + MOE_BLK - 1) // MOE_BLK * MOE_BLK
    pad_end = jnp.cumsum(padded)
    pad_start = pad_end - padded
    dest = jnp.sum(pad_start[None, :] * onehot, axis=1) + rank
    blk_start = jnp.arange(N_BLK, dtype=jnp.int32) * MOE_BLK
    blk_expert = jnp.minimum(jnp.sum((blk_start[:, None] >= pad_end[None, :]).astype(jnp.int32), axis=1), N_EXP - 1)
    n_used = (pad_end[-1] // MOE_BLK).astype(jnp.int32).reshape(1)
    return dest.astype(jnp.int32), blk_expert.astype(jnp.int32), n_used


def _rope_rot(w):
    half, quarter = ROPE // 2, ROPE // 4
    return jnp.concatenate([-w[..., quarter:half], w[..., :quarter], -w[..., half + quarter:], w[..., half:half + quarter]],
                           axis=-1)


def _rope_tables():
    half = ROPE // 2
    inv = ROPE_BASE ** (-(jnp.arange(0, half, 2, dtype=F32) / half))
    t = jnp.arange(S)
    ang_r = (t // GRID_W).astype(F32)[:, None] * inv
    ang_c = (t % GRID_W).astype(F32)[:, None] * inv
    ang = jnp.concatenate([ang_r, ang_r, ang_c, ang_c], axis=-1)
    cos32, sin32 = jnp.cos(ang), jnp.sin(ang)
    cos = jnp.ones((S + TM, LANE), F32).at[:S, NOPE:NOPE + ROPE].set(cos32)
    sin = jnp.zeros((S + TM, LANE), F32).at[:S, NOPE:NOPE + ROPE].set(sin32)
    return cos, sin


def _pack_layer(l, w_in, g_q, w_uq, g_kv, w_ukv, g_v, b_v, w_sp, b_sp, w_br, w_out):
    wi = w_in[l]
    offs = np.cumsum((0, Q_LORA, KV_LORA, ROPE, NA_W, NA_W, NA_W, CM_W, CM_W))
    cq, ckv, kr, naq, nak, nav, u, v = (wi[:, offs[i]:offs[i + 1]] for i in range(8))
    gl = wi[:, offs[8]:]
    zl = jnp.zeros((D, NOPE), F32)
    zr = jnp.zeros((D, LANE - NOPE - ROPE), F32)
    w_in_p = jnp.concatenate([cq, ckv, zl, kr, zr, zl, _rope_rot(kr), zr, naq, nak, nav, u, v, gl], axis=1).astype(BF16)
    uq = w_uq[l].reshape(Q_LORA, H_MLA, QK)
    zq = jnp.zeros((Q_LORA, H_MLA, LANE - QK), F32)
    w_qa = jnp.concatenate([uq, zq], axis=2).reshape(Q_LORA, H_MLA * LANE).astype(BF16)
    w_qb = jnp.concatenate([jnp.zeros((Q_LORA, H_MLA, NOPE), F32), _rope_rot(uq[:, :, NOPE:]), zq], axis=2)
    w_qb = w_qb.reshape(Q_LORA, H_MLA * LANE).astype(BF16)
    ukv = w_ukv[l].reshape(KV_LORA, H_MLA, NOPE + VD)
    w_uk = jnp.concatenate([ukv[:, :, :NOPE], jnp.zeros((KV_LORA, H_MLA, LANE - NOPE), F32)], axis=2)
    w_uk = w_uk.reshape(KV_LORA, H_MLA * LANE).astype(BF16)
    w_uv = ukv[:, :, NOPE:].reshape(KV_LORA, H_MLA * VD).astype(BF16)
    b_sp_t = jnp.repeat(b_sp[l].T, CM_W // CM_G, axis=1)
    return dict(
        w_in=w_in_p, g_q=g_q[l].reshape(1, -1), w_qa=w_qa, w_qb=w_qb, g_kv=g_kv[l].reshape(1, -1), w_uk=w_uk,
        w_uv=w_uv, g_v=g_v[l].reshape(1, -1), b_v=b_v[l].reshape(1, -1), w_sp=w_sp[l].astype(BF16), b_sp=b_sp_t,
        w_br=w_br[l].astype(BF16), w_out=w_out[l].astype(BF16))


def kernel(x, c, ctx, c_ctx, w_mod, b_mod, g_pre_mix, g_post_mix, g_pre_ffn, g_post_ffn, w_in, g_q, w_uq, g_kv, w_ukv, rpb, g_v, b_v, w_sp, b_sp, w_br, w_out, w_ffn1, w_ffn3, w_ffn2, w_router, w_moe1, w_moe3, w_moe2):
    h = jnp.concatenate([x.reshape(NX, D), ctx.reshape(NC, D)], axis=0)
    c_all = jnp.concatenate([c, c_ctx[None, :], jnp.zeros((16 - B - 1, D), F32)], axis=0)
    mod = _modulation(c_all, w_mod, b_mod).reshape(DEPTH, 16, N_MOD, D)
    rope_cos, rope_sin = _rope_tables()
    row1 = lambda a: a.reshape(1, -1)

    for l in range(DEPTH):
        last = l == DEPTH - 1
        wl = _pack_layer(l, w_in, g_q, w_uq, g_kv, w_ukv, g_v, b_v, w_sp, b_sp, w_br, w_out)
        mod_l = mod[l]
        q, k, v, naq, nak, nav, ycm, sg = _inproj(h, mod_l, row1(g_pre_mix[l]), wl, rope_cos, rope_sin)
        ym = _mla_attention(q, k, v, with_ctx=not last)
        yn = _na_attention(naq, nak, nav, _na_bias_table(rpb[l]), with_ctx=not last)
        if l % 2 == 0:
            i = l // 2
            h1, t = _merge(ym, yn, ycm, sg, h, mod_l, row1(g_post_mix[l]), row1(g_pre_ffn[l]), wl, N)
            h = _dense_ffn(t, h1, mod_l, row1(g_post_ffn[l]), w_ffn1[i].astype(BF16), w_ffn3[i].astype(BF16),
                           w_ffn2[i].astype(BF16))
        else:
            i = l // 2
            wr_hi = w_router[i].astype(BF16)
            wr_lo = (w_router[i] - wr_hi.astype(F32)).astype(BF16)
            zpad = jnp.zeros((D, LANE - N_EXP), BF16)
            wr = jnp.concatenate([wr_hi, zpad, wr_lo, zpad], axis=1)
            h1, t, e, w = _merge(ym, yn, ycm, sg, h, mod_l, row1(g_post_mix[l]), row1(g_pre_ffn[l]), wl, NX, wr)
            dest, blk_expert, n_used = _moe_layout(e[:, :2])
            slot_tok = jnp.zeros((N_PAD,), jnp.int32).at[dest].set(jnp.arange(N_ASG, dtype=jnp.int32) // 2)
            x_pad = jnp.take(t, slot_tok, axis=0)
            y = _moe_experts(x_pad, blk_expert, n_used, w_moe1[i].astype(BF16), w_moe3[i].astype(BF16),
                             w_moe2[i].astype(BF16))
            d2 = dest.reshape(NX, 2)
            h = _moe_combine(jnp.take(y, d2[:, 0], axis=0), jnp.take(y, d2[:, 1], axis=0), w, h1, mod_l,
                             row1(g_post_ffn[l]))
    return h[:NX].reshape(B, S, D)
```

```python
import functools

import numpy as np
import jax
import jax.numpy as jnp
from jax import lax
from jax.experimental import pallas as pl
from jax.experimental.pallas import tpu as pltpu

F32 = jnp.float32
BF16 = jnp.bfloat16

D = 1024
B = 8
S = 4096
DEPTH = 2
GRID_W = 64
ROWS = S // GRID_W
C = 256
EPS = 1e-6
NEG = -1e30
LOG2E = 1.4426950408889634
N_MOD = 6

H_MLA = 8
Q_LORA = 384
KV_LORA = 256
NOPE = 64
ROPE = 32
VD = 64
QK = NOPE + ROPE
ROPE_BASE = 10000.0

H_NA = 4
HD_NA = 64
NA_W = H_NA * HD_NA
WIN_R = 8
WIN_C = 16
NA_QROWS = 4
NA_KROWS = 12

CM_G = 4
CM_CHUNK = 128
CM_W = 256

D_FF = 2816
N_EXP = 8
D_FFE = 3584
MOE_BLK = 512

NX = B * S
NC = B * C
N = NX + NC
LANE = 128

TM = 512
TQ = 256
MLA_HPS = 4
FT = 512
N_ASG = NX * 2
N_BLK = N_ASG // MOE_BLK + N_EXP
N_PAD = N_BLK * MOE_BLK

O_CQ, O_CKV, O_KRA, O_KRB, O_NAQ, O_NAK, O_NAV, O_U, O_V, O_GL = 0, 384, 640, 768, 896, 1152, 1408, 1920, 2176, 2432
W_IN_PACKED = O_GL + 3 * D

VMEM_BIG = 56 * 1024 * 1024


def _rms(xf, g):
    return xf * lax.rsqrt(jnp.mean(xf * xf, axis=-1, keepdims=True) + EPS) * g


def _dot(a, b):
    return jnp.dot(a, b, preferred_element_type=F32)


def _dot_nt(a, b):
    return lax.dot_general(a, b, (((1,), (1,)), ((), ())), preferred_element_type=F32)


def _const_spec(shape):
    nd = len(shape)
    return pl.BlockSpec(shape, lambda *_: (0,) * nd, pipeline_mode=pl.Buffered(1))


def _mod_index(tiles_per_batch):
    return lambda i: (jnp.minimum(i // tiles_per_batch, B), 0, 0)


def _mod_kernel(c_ref, w_ref, b_ref, o_ref):
    c = c_ref[...]
    sc = c * jax.nn.sigmoid(c)
    o_ref[0] = jnp.dot(sc, w_ref[0], preferred_element_type=F32, precision=lax.Precision.HIGHEST) + b_ref[0]


def _modulation(c_all, w_mod, b_mod):
    tn = 1536
    return pl.pallas_call(
        _mod_kernel,
        grid=(DEPTH, N_MOD * D // tn),
        in_specs=[
            pl.BlockSpec((16, D), lambda l, j: (0, 0)),
            pl.BlockSpec((1, D, tn), lambda l, j: (l, 0, j)),
            pl.BlockSpec((1, 1, tn), lambda l, j: (l, 0, j)),
        ],
        out_specs=pl.BlockSpec((1, 16, tn), lambda l, j: (l, 0, j)),
        out_shape=jax.ShapeDtypeStruct((DEPTH, 16, N_MOD * D), F32),
        compiler_params=pltpu.CompilerParams(dimension_semantics=("parallel", "parallel"), vmem_limit_bytes=VMEM_BIG),
        name="modulation",
    )(c_all, w_mod, b_mod.reshape(DEPTH, 1, N_MOD * D))


def _inproj_kernel(h_ref, mod_ref, gpre_ref, win_ref, gq_ref, wqa_ref, wqb_ref, gkv_ref, wuk_ref, wuv_ref,
                   cos_ref, sin_ref, gv_ref, bv_ref, wsp_ref, bsp_ref,
                   q_ref, k_ref, v_ref, naq_ref, nak_ref, nav_ref, ycm_ref, sg_ref):
    n = _rms(h_ref[...], gpre_ref[...]) * (1.0 + mod_ref[0, 1:2, :]) + mod_ref[0, 0:1, :]
    nb = n.astype(BF16)

    def proj(a, b):
        return _dot(nb, win_ref[:, a:b])

    cos = cos_ref[...]
    sin = sin_ref[...]
    ones_hi = jnp.where(lax.broadcasted_iota(jnp.int32, (TM, LANE), 1) >= VD, 1.0, 0.0)

    cqn = _rms(proj(O_CQ, O_CKV), gq_ref[...]).astype(BF16)
    qa = _dot(cqn, wqa_ref[...])
    qb = _dot(cqn, wqb_ref[...])
    for hh in range(H_MLA):
        sl = slice(hh * LANE, (hh + 1) * LANE)
        q_ref[hh] = ((qa[:, sl] * cos + qb[:, sl] * sin) * (QK ** -0.5 * LOG2E)).astype(BF16)
    ckvn = _rms(proj(O_CKV, O_KRA), gkv_ref[...]).astype(BF16)
    krp = proj(O_KRA, O_KRB) * cos + proj(O_KRB, O_NAQ) * sin
    kk = _dot(ckvn, wuk_ref[...])
    for hh in range(H_MLA):
        k_ref[hh] = (kk[:, hh * LANE:(hh + 1) * LANE] + krp).astype(BF16)
    vv = _dot(ckvn, wuv_ref[...])
    for hh in range(H_MLA):
        v_ref[hh] = (vv[:, hh * LANE:(hh + 1) * LANE] + ones_hi).astype(BF16)

    naq = proj(O_NAQ, O_NAK) * (HD_NA ** -0.5 * LOG2E)
    nak = proj(O_NAK, O_NAV)
    for p in range(H_NA // 2):
        sl = slice(p * LANE, (p + 1) * LANE)
        naq_ref[p] = naq[:, sl].astype(BF16)
        nak_ref[p] = nak[:, sl].astype(BF16)
    nav = proj(O_NAV, O_U)
    for hh in range(H_NA):
        nav_ref[hh] = (nav[:, hh * LANE:(hh + 1) * LANE] + ones_hi).astype(BF16)

    u = proj(O_U, O_V)
    v = proj(O_V, O_GL)
    mu = jnp.mean(v, axis=-1, keepdims=True)
    vc = v - mu
    vn = (vc * lax.rsqrt(jnp.mean(vc * vc, axis=-1, keepdims=True) + EPS) * gv_ref[...] + bv_ref[...]).astype(BF16)
    grp = lax.broadcasted_iota(jnp.int32, (CM_CHUNK, CM_W), 1) // (CM_W // CM_G)
    for ch in range(TM // CM_CHUNK):
        rs = slice(ch * CM_CHUNK, (ch + 1) * CM_CHUNK)
        vch = vn[rs, :]
        sp = _dot(wsp_ref[CM_G - 1], vch)
        for g in range(CM_G - 2, -1, -1):
            sp = jnp.where(grp == g, _dot(wsp_ref[g], vch), sp)
        ycm_ref[rs, :] = (u[rs, :] * (sp + bsp_ref[...])).astype(BF16)

    for j in range(3):
        sg_ref[:, j * D:(j + 1) * D] = jax.nn.sigmoid(proj(O_GL + j * D, O_GL + (j + 1) * D)).astype(BF16)


def _inproj(h, mod_l, gpre, wl, rope_cos, rope_sin):
    tiles_x = S // TM
    n_tiles = N // TM
    pos_idx = lambda i: (jnp.where(i < NX // TM, i % tiles_x, tiles_x), 0)
    row = lambda i: (i, 0)
    hrow = lambda i: (0, i, 0)
    in_specs = [
        pl.BlockSpec((TM, D), row),
        pl.BlockSpec((1, N_MOD, D), _mod_index(tiles_x)),
        _const_spec((1, D)),
        _const_spec((D, W_IN_PACKED)),
        _const_spec((1, Q_LORA)),
        _const_spec((Q_LORA, H_MLA * LANE)),
        _const_spec((Q_LORA, H_MLA * LANE)),
        _const_spec((1, KV_LORA)),
        _const_spec((KV_LORA, H_MLA * LANE)),
        _const_spec((KV_LORA, H_MLA * LANE)),
        pl.BlockSpec((TM, LANE), pos_idx),
        pl.BlockSpec((TM, LANE), pos_idx),
        _const_spec((1, CM_W)),
        _const_spec((1, CM_W)),
        _const_spec((CM_G, CM_CHUNK, CM_CHUNK)),
        _const_spec((CM_CHUNK, CM_W)),
    ]
    heads = lambda n: (jax.ShapeDtypeStruct((n, N, LANE), BF16), pl.BlockSpec((n, TM, LANE), hrow))
    outs = [
        heads(H_MLA), heads(H_MLA), heads(H_MLA), heads(H_NA // 2), heads(H_NA // 2), heads(H_NA),
        (jax.ShapeDtypeStruct((N, CM_W), BF16), pl.BlockSpec((TM, CM_W), row)),
        (jax.ShapeDtypeStruct((N, 3 * D), BF16), pl.BlockSpec((TM, 3 * D), row)),
    ]
    return pl.pallas_call(
        _inproj_kernel,
        grid=(n_tiles,),
        in_specs=in_specs,
        out_specs=[o[1] for o in outs],
        out_shape=[o[0] for o in outs],
        compiler_params=pltpu.CompilerParams(dimension_semantics=("parallel",), vmem_limit_bytes=VMEM_BIG),
        name="inproj",
    )(h, mod_l, gpre, wl["w_in"], wl["g_q"], wl["w_qa"], wl["w_qb"], wl["g_kv"], wl["w_uk"], wl["w_uv"],
      rope_cos, rope_sin, wl["g_v"], wl["b_v"], wl["w_sp"], wl["b_sp"])


def _pair_out(acc0, acc1, lane):
    return jnp.where(lane < VD, acc0 / pltpu.roll(acc0, VD, 1), pltpu.roll(acc1, VD, 1) / acc1)


def _softmax_pv(scores, values):
    m = None
    for s in scores:
        ms = jnp.max(s, axis=-1, keepdims=True)
        m = ms if m is None else jnp.maximum(m, ms)
    acc = None
    for s, v in zip(scores, values):
        part = _dot(jnp.exp2(s - m).astype(BF16), v)
        acc = part if acc is None else acc + part
    return acc


def _mla_kernel(q_ref, kx_ref, kc_ref, vx_ref, vc_ref, o_ref, *, n_qx, with_ctx):
    lane = lax.broadcasted_iota(jnp.int32, (TQ, LANE), 1)

    def step(with_x):
        scores = []
        for hh in range(MLA_HPS):
            q = q_ref[hh]
            scores.append([_dot_nt(q, kc_ref[hh])] + ([_dot_nt(q, kx_ref[hh])] if with_x else []))
        accs = [_softmax_pv(scores[hh], [vc_ref[hh]] + ([vx_ref[hh]] if with_x else [])) for hh in range(MLA_HPS)]
        for p in range(MLA_HPS // 2):
            o_ref[:, p * LANE:(p + 1) * LANE] = _pair_out(accs[2 * p], accs[2 * p + 1], lane).astype(o_ref.dtype)

    if not with_ctx:
        step(True)
        return
    is_x = pl.program_id(2) < n_qx
    pl.when(is_x)(lambda: step(True))
    pl.when(jnp.logical_not(is_x))(lambda: step(False))


def _mla_attention(q, k, v, with_ctx):
    n_qx = S // TQ
    n_q = n_qx + (1 if with_ctx else 0)
    xq = NX // TQ
    qrow = lambda b, q_: jnp.where(q_ < n_qx, b * n_qx + q_, xq + b)
    hps = MLA_HPS
    xkeys = pl.BlockSpec((hps, S, LANE), lambda b, p, q_: (p, b, 0))
    ckeys = pl.BlockSpec((hps, C, LANE), lambda b, p, q_: (p, NX // C + b, 0))
    return pl.pallas_call(
        functools.partial(_mla_kernel, n_qx=n_qx, with_ctx=with_ctx),
        grid=(B, H_MLA // hps, n_q),
        in_specs=[pl.BlockSpec((hps, TQ, LANE), lambda b, p, q_: (p, qrow(b, q_), 0)), xkeys, ckeys, xkeys, ckeys],
        out_specs=pl.BlockSpec((TQ, hps * VD), lambda b, p, q_: (qrow(b, q_), p)),
        out_shape=jax.ShapeDtypeStruct((N if with_ctx else NX, H_MLA * VD), BF16),
        compiler_params=pltpu.CompilerParams(dimension_semantics=("parallel", "parallel", "arbitrary"),
                                             vmem_limit_bytes=VMEM_BIG),
        name="mla_attention",
    )(q, k, k, v, v)


def _na_kernel(q_ref, kx_ref, kc_ref, vx_ref, vc_ref, bias_ref, o_ref, *, n_g, with_ctx):
    lane = lax.broadcasted_iota(jnp.int32, (TQ, LANE), 1)

    def step(windowed):
        if windowed:
            base = jnp.clip(pl.program_id(1) * NA_QROWS - WIN_R // 2, 0, ROWS - NA_KROWS)
            ks = pl.ds(pl.multiple_of(base * GRID_W, GRID_W), NA_KROWS * GRID_W)
        for p in range(H_NA // 2):
            qp = q_ref[p]
            scores = []
            for half in range(2):
                q = jnp.where((lane < HD_NA) == (half == 0), qp, jnp.zeros_like(qp))
                sc = [_dot_nt(q, kc_ref[p])]
                if windowed:
                    sc.append(_dot_nt(q, kx_ref[p, ks, :]) + bias_ref[0, 2 * p + half])
                scores.append(sc)
            accs = []
            for half in range(2):
                hh = 2 * p + half
                accs.append(_softmax_pv(scores[half], [vc_ref[hh]] + ([vx_ref[hh, ks, :]] if windowed else [])))
            o_ref[:, p * LANE:(p + 1) * LANE] = _pair_out(accs[0], accs[1], lane).astype(o_ref.dtype)

    if not with_ctx:
        step(True)
        return
    is_x = pl.program_id(1) < n_g
    pl.when(is_x)(lambda: step(True))
    pl.when(jnp.logical_not(is_x))(lambda: step(False))


def _na_attention(q, k, v, bias, with_ctx):
    n_g = ROWS // NA_QROWS
    n_steps = n_g + (1 if with_ctx else 0)
    xq = NX // TQ
    qrow = lambda b, g: jnp.where(g < n_g, b * n_g + g, xq + b)
    kind = lambda b, g: (jnp.where(g == 0, 0, jnp.where(g >= n_g - 1, 2, 1)), 0, 0, 0)
    hp = H_NA // 2
    return pl.pallas_call(
        functools.partial(_na_kernel, n_g=n_g, with_ctx=with_ctx),
        grid=(B, n_steps),
        in_specs=[
            pl.BlockSpec((hp, TQ, LANE), lambda b, g: (0, qrow(b, g), 0)),
            pl.BlockSpec((hp, S, LANE), lambda b, g: (0, b, 0)),
            pl.BlockSpec((hp, C, LANE), lambda b, g: (0, NX // C + b, 0)),
            pl.BlockSpec((H_NA, S, LANE), lambda b, g: (0, b, 0)),
            pl.BlockSpec((H_NA, C, LANE), lambda b, g: (0, NX // C + b, 0)),
            pl.BlockSpec((1, H_NA, TQ, NA_KROWS * GRID_W), kind),
        ],
        out_specs=pl.BlockSpec((TQ, NA_W), lambda b, g: (qrow(b, g), 0)),
        out_shape=jax.ShapeDtypeStruct((N if with_ctx else NX, NA_W), BF16),
        compiler_params=pltpu.CompilerParams(dimension_semantics=("parallel", "arbitrary"),
                                             vmem_limit_bytes=VMEM_BIG),
        name="na_attention",
    )(q, k, k, v, v, bias)


def _na_bias_table(rpb_l):
    qc = np.arange(GRID_W)
    kc = np.arange(GRID_W)
    ws = np.clip(qc - WIN_C // 2, 0, GRID_W - WIN_C)
    col_ok = (kc[None, :] >= ws[:, None]) & (kc[None, :] < ws[:, None] + WIN_C)
    dc = np.clip(kc[None, :] - qc[:, None] + WIN_C - 1, 0, 2 * WIN_C - 2)
    sel_c = (dc[None] == np.arange(2 * WIN_C - 1)[:, None, None]).astype(np.float32)
    n_g = ROWS // NA_QROWS
    sel_r, oks = [], []
    for g in (0, 1, n_g - 1):
        base = int(np.clip(g * NA_QROWS - WIN_R // 2, 0, ROWS - NA_KROWS))
        r = g * NA_QROWS + np.arange(NA_QROWS)
        rs = np.clip(r - WIN_R // 2, 0, ROWS - WIN_R)
        kr = base + np.arange(NA_KROWS)
        row_ok = (kr[None, :] >= rs[:, None]) & (kr[None, :] < rs[:, None] + WIN_R)
        ri = kr[None, :] - r[:, None] + WIN_R - 1
        sel_r.append(((ri[:, :, None] == np.arange(2 * WIN_R - 1)) & row_ok[:, :, None]).astype(np.float32))
        oks.append(row_ok[:, None, :, None] & col_ok[None, :, None, :])
    hi = lax.Precision.HIGHEST
    by_col = jnp.einsum("hrd,dqk->hrqk", rpb_l * LOG2E, jnp.asarray(sel_c), precision=hi)
    bias = jnp.einsum("glar,hrqk->ghlqak", jnp.asarray(np.stack(sel_r)), by_col, precision=hi)
    bias = jnp.where(jnp.asarray(np.stack(oks))[:, None], bias, NEG)
    return bias.reshape(3, H_NA, TQ, NA_KROWS * GRID_W).astype(F32)


def _merge_core(ym_ref, yn_ref, yc_ref, sg_ref, h_ref, mod_ref, gpost_ref, gffn_ref, wbr_ref, wout_ref):
    o1, o2 = H_MLA * VD, H_MLA * VD + NA_W
    z = (sg_ref[:, 0:D].astype(F32) * _dot(ym_ref[...], wbr_ref[0:o1, :])
         + sg_ref[:, D:2 * D].astype(F32) * _dot(yn_ref[...], wbr_ref[o1:o2, :])
         + sg_ref[:, 2 * D:3 * D].astype(F32) * _dot(yc_ref[...], wbr_ref[o2:, :]))
    o = _dot(z.astype(BF16), wout_ref[...])
    h1 = h_ref[...] + mod_ref[0, 2:3, :] * _rms(o, gpost_ref[...])
    t = _rms(h1, gffn_ref[...]) * (1.0 + mod_ref[0, 4:5, :]) + mod_ref[0, 3:4, :]
    return h1, t


def _merge_kernel(ym_ref, yn_ref, yc_ref, sg_ref, h_ref, mod_ref, gpost_ref, gffn_ref, wbr_ref, wout_ref,
                  h1_ref, t_ref):
    h1, t = _merge_core(ym_ref, yn_ref, yc_ref, sg_ref, h_ref, mod_ref, gpost_ref, gffn_ref, wbr_ref, wout_ref)
    h1_ref[...] = h1
    t_ref[...] = t.astype(BF16)


def _merge_route_kernel(ym_ref, yn_ref, yc_ref, sg_ref, h_ref, mod_ref, gpost_ref, gffn_ref, wbr_ref, wout_ref,
                        wr_ref, h1_ref, t_ref, e_ref, w_ref, cnt_ref):
    h1, t = _merge_core(ym_ref, yn_ref, yc_ref, sg_ref, h_ref, mod_ref, gpost_ref, gffn_ref, wbr_ref, wout_ref)
    h1_ref[...] = h1
    t_hi = t.astype(BF16)
    t_ref[...] = t_hi
    t_lo = (t - t_hi.astype(F32)).astype(BF16)
    hi = _dot(t_hi, wr_ref[...])
    logits = hi[:, :LANE] + hi[:, LANE:] + _dot(t_lo, wr_ref[:, :LANE])
    lane = lax.broadcasted_iota(jnp.int32, logits.shape, 1)
    logits = jnp.where(lane < N_EXP, logits, -jnp.inf)
    l1 = jnp.max(logits, axis=-1, keepdims=True)
    e1 = jnp.min(jnp.where(logits == l1, lane, LANE), axis=-1, keepdims=True)
    rest = jnp.where(lane == e1, -jnp.inf, logits)
    l2 = jnp.max(rest, axis=-1, keepdims=True)
    e2 = jnp.min(jnp.where(rest == l2, lane, LANE), axis=-1, keepdims=True)
    ex = jnp.exp(l2 - l1)
    den = 1.0 + ex
    w_ref[...] = jnp.where(lane == 0, 1.0 / den, jnp.where(lane == 1, ex / den, 0.0))

    @pl.when(pl.program_id(0) == 0)
    def _():
        cnt_ref[...] = jnp.zeros(cnt_ref.shape, F32)

    oh1 = lane == e1
    oh2 = lane == e2
    both = jnp.where(oh1 | oh2, 1.0, 0.0)
    r_i = lax.broadcasted_iota(jnp.int32, (TM, TM), 0)
    c_i = lax.broadcasted_iota(jnp.int32, (TM, TM), 1)
    tri = jnp.where(c_i < r_i, 1.0, 0.0).astype(BF16)
    run = cnt_ref[0:1, :]
    before = _dot(tri, both.astype(BF16)) + run
    rank1 = jnp.sum(jnp.where(oh1, before, 0.0), axis=-1, keepdims=True).astype(jnp.int32)
    rank2 = jnp.sum(jnp.where(oh2, before, 0.0), axis=-1, keepdims=True).astype(jnp.int32)
    e_ref[...] = jnp.where(lane == 0, e1, jnp.where(lane == 1, e2, jnp.where(lane == 2, rank1,
                                                                               jnp.where(lane == 3, rank2, 0))))
    cnt_ref[...] = jnp.broadcast_to(run + jnp.sum(both, axis=0, keepdims=True), cnt_ref.shape)


def _merge(ym, yn, yc, sg, h, mod_l, gpost, gffn, wl, n_rows, w_router=None):
    row = lambda i: (i, 0)
    in_specs = [
        pl.BlockSpec((TM, H_MLA * VD), row),
        pl.BlockSpec((TM, NA_W), row),
        pl.BlockSpec((TM, CM_W), row),
        pl.BlockSpec((TM, 3 * D), row),
        pl.BlockSpec((TM, D), row),
        pl.BlockSpec((1, N_MOD, D), _mod_index(S // TM)),
        _const_spec((1, D)),
        _const_spec((1, D)),
        _const_spec((D, D)),
        _const_spec((D, D)),
    ]
    out_shape = [jax.ShapeDtypeStruct((n_rows, D), F32), jax.ShapeDtypeStruct((n_rows, D), BF16)]
    out_specs = [pl.BlockSpec((TM, D), row), pl.BlockSpec((TM, D), row)]
    args = [ym, yn, yc, sg, h, mod_l, gpost, gffn, wl["w_br"], wl["w_out"]]
    if w_router is None:
        body, name, sem = _merge_kernel, "merge", "parallel"
    else:
        body, name, sem = _merge_route_kernel, "merge_route", "arbitrary"
        in_specs.append(_const_spec((D, 2 * LANE)))
        args.append(w_router)
        out_shape += [jax.ShapeDtypeStruct((n_rows, LANE), jnp.int32), jax.ShapeDtypeStruct((n_rows, LANE), F32),
                      jax.ShapeDtypeStruct((8, LANE), F32)]
        out_specs += [pl.BlockSpec((TM, LANE), row), pl.BlockSpec((TM, LANE), row),
                      pl.BlockSpec((8, LANE), lambda i: (0, 0))]
    return pl.pallas_call(
        body,
        grid=(n_rows // TM,),
        in_specs=in_specs,
        out_specs=out_specs,
        out_shape=out_shape,
        compiler_params=pltpu.CompilerParams(dimension_semantics=(sem,), vmem_limit_bytes=VMEM_BIG),
        name=name,
    )(*args)


def _ffn_kernel(t_ref, h1_ref, mod_ref, g_ref, w1_ref, w3_ref, w2_ref, o_ref):
    t = t_ref[...]
    half = D_FF // 2
    f = None
    for j in range(2):
        sl = slice(j * half, (j + 1) * half)
        a = _dot(t, w1_ref[:, sl])
        b = _dot(t, w3_ref[:, sl])
        part = _dot((a * jax.nn.sigmoid(a) * b).astype(BF16), w2_ref[sl, :])
        f = part if f is None else f + part
    o_ref[...] = h1_ref[...] + mod_ref[0, 5:6, :] * _rms(f, g_ref[...])


def _dense_ffn(t, h1, mod_l, g, w1, w3, w2):
    row = lambda i: (i, 0)
    return pl.pallas_call(
        _ffn_kernel,
        grid=(N // TM,),
        in_specs=[
            pl.BlockSpec((TM, D), row),
            pl.BlockSpec((TM, D), row),
            pl.BlockSpec((1, N_MOD, D), _mod_index(S // TM)),
            _const_spec((1, D)),
            _const_spec((D, D_FF)),
            _const_spec((D, D_FF)),
            _const_spec((D_FF, D)),
        ],
        out_specs=pl.BlockSpec((TM, D), row),
        out_shape=jax.ShapeDtypeStruct((N, D), F32),
        compiler_params=pltpu.CompilerParams(dimension_semantics=("parallel",), vmem_limit_bytes=VMEM_BIG),
        name="dense_ffn",
    )(t, h1, mod_l, g, w1, w3, w2)


def _moe_kernel(be_ref, nu_ref, x_ref, w1_ref, w3_ref, w2_ref, y_ref):
    used = pl.program_id(0) < nu_ref[0]

    @pl.when(used)
    def _():
        x = x_ref[...]
        for f in range(D_FFE // FT):
            sl = slice(f * FT, (f + 1) * FT)
            a = _dot(x, w1_ref[0, :, sl])
            b = _dot(x, w3_ref[0, :, sl])
            part = _dot((a * jax.nn.sigmoid(a) * b).astype(BF16), w2_ref[0, sl, :])
            if f == 0:
                y_ref[...] = part
            else:
                y_ref[...] += part

    @pl.when(jnp.logical_not(used))
    def _():
        y_ref[...] = jnp.zeros(y_ref.shape, y_ref.dtype)


def _moe_experts(x_pad, blk_expert, n_used, w1, w3, w2):
    wspec = lambda shape: pl.BlockSpec(shape, lambda j, be, nu: (be[j], 0, 0), pipeline_mode=pl.Buffered(1))
    grid_spec = pltpu.PrefetchScalarGridSpec(
        num_scalar_prefetch=2,
        grid=(N_BLK,),
        in_specs=[
            pl.BlockSpec((MOE_BLK, D), lambda j, be, nu: (j, 0)),
            wspec((1, D, D_FFE)),
            wspec((1, D, D_FFE)),
            wspec((1, D_FFE, D)),
        ],
        out_specs=pl.BlockSpec((MOE_BLK, D), lambda j, be, nu: (j, 0)),
    )
    return pl.pallas_call(
        _moe_kernel,
        grid_spec=grid_spec,
        out_shape=jax.ShapeDtypeStruct((N_PAD, D), F32),
        compiler_params=pltpu.CompilerParams(dimension_semantics=("arbitrary",), vmem_limit_bytes=VMEM_BIG),
        name="moe_experts",
    )(blk_expert, n_used, x_pad, w1, w3, w2)


def _combine_kernel(ya_ref, yb_ref, w_ref, h1_ref, mod_ref, g_ref, o_ref):
    f = w_ref[:, 0:1] * ya_ref[...] + w_ref[:, 1:2] * yb_ref[...]
    o_ref[...] = h1_ref[...] + mod_ref[0, 5:6, :] * _rms(f, g_ref[...])


def _moe_combine(ya, yb, w, h1, mod_l, g):
    row = lambda i: (i, 0)
    return pl.pallas_call(
        _combine_kernel,
        grid=(NX // TM,),
        in_specs=[
            pl.BlockSpec((TM, D), row),
            pl.BlockSpec((TM, D), row),
            pl.BlockSpec((TM, LANE), row),
            pl.BlockSpec((TM, D), row),
            pl.BlockSpec((1, N_MOD, D), _mod_index(S // TM)),
            _const_spec((1, D)),
        ],
        out_specs=pl.BlockSpec((TM, D), row),
        out_shape=jax.ShapeDtypeStruct((NX, D), F32),
        compiler_params=pltpu.CompilerParams(dimension_semantics=("parallel",), vmem_limit_bytes=VMEM_BIG),
        name="moe_combine",
    )(ya, yb, w, h1, mod_l, g)


def _moe_layout(e, rank, counts):
    padded = (counts + MOE_BLK - 1) // MOE_BLK * MOE_BLK
    pad_end = jnp.cumsum(padded)
    pad_start = pad_end - padded
    onehot = e[:, :, None] == jnp.arange(N_EXP, dtype=jnp.int32)
    dest = jnp.sum(jnp.where(onehot, pad_start, 0), axis=-1) + rank
    blk_start = jnp.arange(N_BLK, dtype=jnp.int32) * MOE_BLK
    blk_expert = jnp.minimum(jnp.sum((blk_start[:, None] >= pad_end[None, :]).astype(jnp.int32), axis=1), N_EXP - 1)
    n_used = (pad_end[-1] // MOE_BLK).astype(jnp.int32).reshape(1)
    return dest.astype(jnp.int32), blk_expert.astype(jnp.int32), n_used


def _rope_rot(w):
    half, quarter = ROPE // 2, ROPE // 4
    return jnp.concatenate([-w[..., quarter:half], w[..., :quarter], -w[..., half + quarter:], w[..., half:half + quarter]],
                           axis=-1)


def _rope_tables():
    half = ROPE // 2
    inv = ROPE_BASE ** (-(jnp.arange(0, half, 2, dtype=F32) / half))
    t = jnp.arange(S)
    ang_r = (t // GRID_W).astype(F32)[:, None] * inv
    ang_c = (t % GRID_W).astype(F32)[:, None] * inv
    ang = jnp.concatenate([ang_r, ang_r, ang_c, ang_c], axis=-1)
    cos32, sin32 = jnp.cos(ang), jnp.sin(ang)
    cos = jnp.ones((S + TM, LANE), F32).at[:S, NOPE:NOPE + ROPE].set(cos32)
    sin = jnp.zeros((S + TM, LANE), F32).at[:S, NOPE:NOPE + ROPE].set(sin32)
    return cos, sin


def _pad_heads(w, n_heads, width):
    w = w.reshape(w.shape[0], n_heads, width)
    return jnp.concatenate([w, jnp.zeros((w.shape[0], n_heads, LANE - width), w.dtype)], axis=2).reshape(w.shape[0], -1)


def _pack_layer(l, w_in, g_q, w_uq, g_kv, w_ukv, g_v, b_v, w_sp, b_sp, w_br, w_out):
    wi = w_in[l]
    offs = np.cumsum((0, Q_LORA, KV_LORA, ROPE, NA_W, NA_W, NA_W, CM_W, CM_W))
    cq, ckv, kr, naq, nak, nav, u, v = (wi[:, offs[i]:offs[i + 1]] for i in range(8))
    gl = wi[:, offs[8]:]
    zl = jnp.zeros((D, NOPE), F32)
    zr = jnp.zeros((D, LANE - NOPE - ROPE), F32)
    w_in_p = jnp.concatenate([cq, ckv, zl, kr, zr, zl, _rope_rot(kr), zr, naq, nak, _pad_heads(nav, H_NA, HD_NA),
                              u, v, gl], axis=1).astype(BF16)
    uq = w_uq[l].reshape(Q_LORA, H_MLA, QK)
    zq = jnp.zeros((Q_LORA, H_MLA, LANE - QK), F32)
    w_qa = jnp.concatenate([uq, zq], axis=2).reshape(Q_LORA, H_MLA * LANE).astype(BF16)
    w_qb = jnp.concatenate([jnp.zeros((Q_LORA, H_MLA, NOPE), F32), _rope_rot(uq[:, :, NOPE:]), zq], axis=2)
    w_qb = w_qb.reshape(Q_LORA, H_MLA * LANE).astype(BF16)
    ukv = w_ukv[l].reshape(KV_LORA, H_MLA, NOPE + VD)
    w_uk = _pad_heads(ukv[:, :, :NOPE].reshape(KV_LORA, -1), H_MLA, NOPE).astype(BF16)
    w_uv = _pad_heads(ukv[:, :, NOPE:].reshape(KV_LORA, -1), H_MLA, VD).astype(BF16)
    b_sp_t = jnp.repeat(b_sp[l].T, CM_W // CM_G, axis=1)
    return dict(
        w_in=w_in_p, g_q=g_q[l].reshape(1, -1), w_qa=w_qa, w_qb=w_qb, g_kv=g_kv[l].reshape(1, -1), w_uk=w_uk,
        w_uv=w_uv, g_v=g_v[l].reshape(1, -1), b_v=b_v[l].reshape(1, -1), w_sp=w_sp[l].astype(BF16), b_sp=b_sp_t,
        w_br=w_br[l].astype(BF16), w_out=w_out[l].astype(BF16))


def kernel(x, c, ctx, c_ctx, w_mod, b_mod, g_pre_mix, g_post_mix, g_pre_ffn, g_post_ffn, w_in, g_q, w_uq, g_kv, w_ukv, rpb, g_v, b_v, w_sp, b_sp, w_br, w_out, w_ffn1, w_ffn3, w_ffn2, w_router, w_moe1, w_moe3, w_moe2):
    assert DEPTH == 2, "layer 0 is the dense layer with context outputs, layer 1 the last (expert) layer"
    h = jnp.concatenate([x.reshape(NX, D), ctx.reshape(NC, D)], axis=0)
    c_all = jnp.concatenate([c, c_ctx[None, :], jnp.zeros((16 - B - 1, D), F32)], axis=0)
    mod = _modulation(c_all, w_mod, b_mod).reshape(DEPTH, 16, N_MOD, D)
    rope_cos, rope_sin = _rope_tables()
    row1 = lambda a: a.reshape(1, -1)

    for l in range(DEPTH):
        last = l == DEPTH - 1
        wl = _pack_layer(l, w_in, g_q, w_uq, g_kv, w_ukv, g_v, b_v, w_sp, b_sp, w_br, w_out)
        mod_l = mod[l]
        q, k, v, naq, nak, nav, ycm, sg = _inproj(h, mod_l, row1(g_pre_mix[l]), wl, rope_cos, rope_sin)
        ym = _mla_attention(q, k, v, with_ctx=not last)
        yn = _na_attention(naq, nak, nav, _na_bias_table(rpb[l]), with_ctx=not last)
        i = l // 2
        if not last:
            h1, t = _merge(ym, yn, ycm, sg, h, mod_l, row1(g_post_mix[l]), row1(g_pre_ffn[l]), wl, N)
            h = _dense_ffn(t, h1, mod_l, row1(g_post_ffn[l]), w_ffn1[i].astype(BF16), w_ffn3[i].astype(BF16),
                           w_ffn2[i].astype(BF16))
        else:
            wr_hi = w_router[i].astype(BF16)
            wr_lo = (w_router[i] - wr_hi.astype(F32)).astype(BF16)
            zpad = jnp.zeros((D, LANE - N_EXP), BF16)
            wr = jnp.concatenate([wr_hi, zpad, wr_lo, zpad], axis=1)
            h1, t, e, w, cnt = _merge(ym, yn, ycm, sg, h, mod_l, row1(g_post_mix[l]), row1(g_pre_ffn[l]), wl, NX, wr)
            dest, blk_expert, n_used = _moe_layout(e[:, 0:2], e[:, 2:4], cnt[0, :N_EXP].astype(jnp.int32))
            slot_tok = jnp.zeros((N_PAD,), jnp.int32).at[dest.reshape(N_ASG)].set(
                jnp.arange(N_ASG, dtype=jnp.int32) // 2)
            x_pad = jnp.take(t, slot_tok, axis=0)
            y = _moe_experts(x_pad, blk_expert, n_used, w_moe1[i].astype(BF16), w_moe3[i].astype(BF16),
                             w_moe2[i].astype(BF16))
            h = _moe_combine(jnp.take(y, dest[:, 0], axis=0), jnp.take(y, dest[:, 1], axis=0), w, h1, mod_l,
                             row1(g_post_ffn[l]))
    return h[:NX].reshape(B, S, D)
```

```python
import functools

import numpy as np
import jax
import jax.numpy as jnp
from jax import lax
from jax.experimental import pallas as pl
from jax.experimental.pallas import tpu as pltpu

F32 = jnp.float32
BF16 = jnp.bfloat16

D = 1024
B = 8
S = 4096
DEPTH = 2
GRID_W = 64
ROWS = S // GRID_W
C = 256
EPS = 1e-6
NEG = -1e30
LOG2E = 1.4426950408889634
N_MOD = 6

H_MLA = 8
Q_LORA = 384
KV_LORA = 256
NOPE = 64
ROPE = 32
VD = 64
QK = NOPE + ROPE
ROPE_BASE = 10000.0

H_NA = 4
HD_NA = 64
NA_W = H_NA * HD_NA
WIN_R = 8
WIN_C = 16
NA_QROWS = 4
NA_KROWS = 12

CM_G = 4
CM_CHUNK = 128
CM_W = 256

D_FF = 2816
N_EXP = 8
D_FFE = 3584
MOE_BLK = 512

NX = B * S
NC = B * C
N = NX + NC
LANE = 128

TM = 512
TQ = 256
MLA_HPS = 4
FT = 512
N_ASG = NX * 2
N_BLK = N_ASG // MOE_BLK + N_EXP
N_PAD = N_BLK * MOE_BLK

O_CQ, O_CKV, O_KRA, O_KRB, O_NAQ, O_NAK, O_NAV, O_U, O_V, O_GL = 0, 384, 640, 768, 896, 1152, 1408, 1920, 2176, 2432
W_IN_PACKED = O_GL + 3 * D

VMEM_BIG = 56 * 1024 * 1024


def _rms(xf, g):
    return xf * lax.rsqrt(jnp.mean(xf * xf, axis=-1, keepdims=True) + EPS) * g


def _dot(a, b):
    return jnp.dot(a, b, preferred_element_type=F32)


def _dot_nt(a, b):
    return lax.dot_general(a, b, (((1,), (1,)), ((), ())), preferred_element_type=F32)


def _const_spec(shape):
    nd = len(shape)
    return pl.BlockSpec(shape, lambda *_: (0,) * nd, pipeline_mode=pl.Buffered(1))


def _mod_index(tiles_per_batch):
    return lambda i: (jnp.minimum(i // tiles_per_batch, B), 0, 0)


def _stream_specs(split):
    nxt = NX // TM
    off = nxt if split else 0
    return [pl.BlockSpec((TM, D), lambda i: (jnp.minimum(i, nxt - 1), 0)),
            pl.BlockSpec((TM, D), lambda i: (jnp.maximum(i, nxt) - off, 0))]


def _stream_tile(ha_ref, hb_ref):
    return jnp.where(pl.program_id(0) < NX // TM, ha_ref[...], hb_ref[...])


def _mod_kernel(c_ref, w_ref, b_ref, o_ref):
    c = c_ref[...]
    sc = c * jax.nn.sigmoid(c)
    o_ref[0] = jnp.dot(sc, w_ref[0], preferred_element_type=F32, precision=lax.Precision.HIGHEST) + b_ref[0]


def _modulation(c_all, w_mod, b_mod):
    tn = 1536
    return pl.pallas_call(
        _mod_kernel,
        grid=(DEPTH, N_MOD * D // tn),
        in_specs=[
            pl.BlockSpec((16, D), lambda l, j: (0, 0)),
            pl.BlockSpec((1, D, tn), lambda l, j: (l, 0, j)),
            pl.BlockSpec((1, 1, tn), lambda l, j: (l, 0, j)),
        ],
        out_specs=pl.BlockSpec((1, 16, tn), lambda l, j: (l, 0, j)),
        out_shape=jax.ShapeDtypeStruct((DEPTH, 16, N_MOD * D), F32),
        compiler_params=pltpu.CompilerParams(dimension_semantics=("parallel", "parallel"), vmem_limit_bytes=VMEM_BIG),
        name="modulation",
    )(c_all, w_mod, b_mod.reshape(DEPTH, 1, N_MOD * D))


def _inproj_kernel(ha_ref, hb_ref, mod_ref, gpre_ref, win_ref, gq_ref, wqa_ref, wqb_ref, gkv_ref, wuk_ref, wuv_ref,
                   cos_ref, sin_ref, gv_ref, bv_ref, wsp_ref, bsp_ref,
                   q_ref, k_ref, v_ref, naq_ref, nak_ref, nav_ref, ycm_ref, sg_ref):
    n = _rms(_stream_tile(ha_ref, hb_ref), gpre_ref[...]) * (1.0 + mod_ref[0, 1:2, :]) + mod_ref[0, 0:1, :]
    nb = n.astype(BF16)

    def proj(a, b):
        return _dot(nb, win_ref[:, a:b])

    cos = cos_ref[...]
    sin = sin_ref[...]
    ones_hi = jnp.where(lax.broadcasted_iota(jnp.int32, (TM, LANE), 1) >= VD, 1.0, 0.0)

    cqn = _rms(proj(O_CQ, O_CKV), gq_ref[...]).astype(BF16)
    qa = _dot(cqn, wqa_ref[...])
    qb = _dot(cqn, wqb_ref[...])
    for hh in range(H_MLA):
        sl = slice(hh * LANE, (hh + 1) * LANE)
        q_ref[hh] = ((qa[:, sl] * cos + qb[:, sl] * sin) * (QK ** -0.5 * LOG2E)).astype(BF16)
    ckvn = _rms(proj(O_CKV, O_KRA), gkv_ref[...]).astype(BF16)
    krp = proj(O_KRA, O_KRB) * cos + proj(O_KRB, O_NAQ) * sin
    kk = _dot(ckvn, wuk_ref[...])
    for hh in range(H_MLA):
        k_ref[hh] = (kk[:, hh * LANE:(hh + 1) * LANE] + krp).astype(BF16)
    vv = _dot(ckvn, wuv_ref[...])
    for hh in range(H_MLA):
        v_ref[hh] = (vv[:, hh * LANE:(hh + 1) * LANE] + ones_hi).astype(BF16)

    naq = proj(O_NAQ, O_NAK) * (HD_NA ** -0.5 * LOG2E)
    nak = proj(O_NAK, O_NAV)
    for p in range(H_NA // 2):
        sl = slice(p * LANE, (p + 1) * LANE)
        naq_ref[p] = naq[:, sl].astype(BF16)
        nak_ref[p] = nak[:, sl].astype(BF16)
    nav = proj(O_NAV, O_U)
    for hh in range(H_NA):
        nav_ref[hh] = (nav[:, hh * LANE:(hh + 1) * LANE] + ones_hi).astype(BF16)

    u = proj(O_U, O_V)
    v = proj(O_V, O_GL)
    mu = jnp.mean(v, axis=-1, keepdims=True)
    vc = v - mu
    vn = (vc * lax.rsqrt(jnp.mean(vc * vc, axis=-1, keepdims=True) + EPS) * gv_ref[...] + bv_ref[...]).astype(BF16)
    grp = lax.broadcasted_iota(jnp.int32, (CM_CHUNK, CM_W), 1) // (CM_W // CM_G)
    for ch in range(TM // CM_CHUNK):
        rs = slice(ch * CM_CHUNK, (ch + 1) * CM_CHUNK)
        vch = vn[rs, :]
        sp = _dot(wsp_ref[CM_G - 1], vch)
        for g in range(CM_G - 2, -1, -1):
            sp = jnp.where(grp == g, _dot(wsp_ref[g], vch), sp)
        ycm_ref[rs, :] = (u[rs, :] * (sp + bsp_ref[...])).astype(BF16)

    for j in range(3):
        sg_ref[:, j * D:(j + 1) * D] = jax.nn.sigmoid(proj(O_GL + j * D, O_GL + (j + 1) * D)).astype(BF16)


def _inproj(h_parts, mod_l, gpre, wl, rope_cos, rope_sin):
    tiles_x = S // TM
    n_tiles = N // TM
    pos_idx = lambda i: (jnp.where(i < NX // TM, i % tiles_x, tiles_x), 0)
    row = lambda i: (i, 0)
    hrow = lambda i: (0, i, 0)
    in_specs = _stream_specs(h_parts[0] is not h_parts[1]) + [
        pl.BlockSpec((1, N_MOD, D), _mod_index(tiles_x)),
        _const_spec((1, D)),
        _const_spec((D, W_IN_PACKED)),
        _const_spec((1, Q_LORA)),
        _const_spec((Q_LORA, H_MLA * LANE)),
        _const_spec((Q_LORA, H_MLA * LANE)),
        _const_spec((1, KV_LORA)),
        _const_spec((KV_LORA, H_MLA * LANE)),
        _const_spec((KV_LORA, H_MLA * LANE)),
        pl.BlockSpec((TM, LANE), pos_idx),
        pl.BlockSpec((TM, LANE), pos_idx),
        _const_spec((1, CM_W)),
        _const_spec((1, CM_W)),
        _const_spec((CM_G, CM_CHUNK, CM_CHUNK)),
        _const_spec((CM_CHUNK, CM_W)),
    ]
    heads = lambda n: (jax.ShapeDtypeStruct((n, N, LANE), BF16), pl.BlockSpec((n, TM, LANE), hrow))
    outs = [
        heads(H_MLA), heads(H_MLA), heads(H_MLA), heads(H_NA // 2), heads(H_NA // 2), heads(H_NA),
        (jax.ShapeDtypeStruct((N, CM_W), BF16), pl.BlockSpec((TM, CM_W), row)),
        (jax.ShapeDtypeStruct((N, 3 * D), BF16), pl.BlockSpec((TM, 3 * D), row)),
    ]
    return pl.pallas_call(
        _inproj_kernel,
        grid=(n_tiles,),
        in_specs=in_specs,
        out_specs=[o[1] for o in outs],
        out_shape=[o[0] for o in outs],
        compiler_params=pltpu.CompilerParams(dimension_semantics=("parallel",), vmem_limit_bytes=VMEM_BIG),
        name="inproj",
    )(*h_parts, mod_l, gpre, wl["w_in"], wl["g_q"], wl["w_qa"], wl["w_qb"], wl["g_kv"], wl["w_uk"], wl["w_uv"],
      rope_cos, rope_sin, wl["g_v"], wl["b_v"], wl["w_sp"], wl["b_sp"])


def _pair_out(acc0, acc1, lane):
    return jnp.where(lane < VD, acc0 / pltpu.roll(acc0, VD, 1), pltpu.roll(acc1, VD, 1) / acc1)


def _mla_kernel(q_ref, kx_ref, kc_ref, vx_ref, vc_ref, o_ref, s_scr, *, n_qx, with_ctx):
    lane = lax.broadcasted_iota(jnp.int32, (TQ, LANE), 1)

    def step(with_x):
        n_k = C + (S if with_x else 0)
        for hh in range(MLA_HPS):
            q = q_ref[hh]
            s_scr[hh, :, 0:C] = _dot_nt(q, kc_ref[hh])
            if with_x:
                s_scr[hh, :, C:] = _dot_nt(q, kx_ref[hh])
        accs = []
        for hh in range(MLA_HPS):
            m = jnp.max(s_scr[hh, :, 0:n_k], axis=-1, keepdims=True)
            acc = _dot(jnp.exp2(s_scr[hh, :, 0:C] - m).astype(BF16), vc_ref[hh])
            if with_x:
                acc = acc + _dot(jnp.exp2(s_scr[hh, :, C:] - m).astype(BF16), vx_ref[hh])
            accs.append(acc)
        for p in range(MLA_HPS // 2):
            o_ref[:, p * LANE:(p + 1) * LANE] = _pair_out(accs[2 * p], accs[2 * p + 1], lane).astype(o_ref.dtype)

    if not with_ctx:
        step(True)
        return
    is_x = pl.program_id(2) < n_qx
    pl.when(is_x)(lambda: step(True))
    pl.when(jnp.logical_not(is_x))(lambda: step(False))


def _mla_attention(q, k, v, with_ctx):
    n_qx = S // TQ
    n_q = n_qx + (1 if with_ctx else 0)
    xq = NX // TQ
    qrow = lambda b, q_: jnp.where(q_ < n_qx, b * n_qx + q_, xq + b)
    hps = MLA_HPS
    xkeys = pl.BlockSpec((hps, S, LANE), lambda b, p, q_: (p, b, 0))
    ckeys = pl.BlockSpec((hps, C, LANE), lambda b, p, q_: (p, NX // C + b, 0))
    return pl.pallas_call(
        functools.partial(_mla_kernel, n_qx=n_qx, with_ctx=with_ctx),
        grid=(B, H_MLA // hps, n_q),
        in_specs=[pl.BlockSpec((hps, TQ, LANE), lambda b, p, q_: (p, qrow(b, q_), 0)), xkeys, ckeys, xkeys, ckeys],
        out_specs=pl.BlockSpec((TQ, hps * VD), lambda b, p, q_: (qrow(b, q_), p)),
        out_shape=jax.ShapeDtypeStruct((N if with_ctx else NX, H_MLA * VD), BF16),
        scratch_shapes=[pltpu.VMEM((hps, TQ, C + S), F32)],
        compiler_params=pltpu.CompilerParams(dimension_semantics=("parallel", "parallel", "arbitrary"),
                                             vmem_limit_bytes=VMEM_BIG),
        name="mla_attention",
    )(q, k, k, v, v)


def _na_kernel(q_ref, kx_ref, kc_ref, vx_ref, vc_ref, bias_ref, o_ref, s_scr, *, n_g, with_ctx):
    lane = lax.broadcasted_iota(jnp.int32, (TQ, LANE), 1)

    def step(windowed):
        n_w = NA_KROWS * GRID_W
        n_k = C + (n_w if windowed else 0)
        if windowed:
            base = jnp.clip(pl.program_id(1) * NA_QROWS - WIN_R // 2, 0, ROWS - NA_KROWS)
            ks = pl.ds(pl.multiple_of(base * GRID_W, GRID_W), n_w)
        for hh in range(H_NA):
            p, half = divmod(hh, 2)
            qp = q_ref[p]
            q = jnp.where((lane < HD_NA) == (half == 0), qp, jnp.zeros_like(qp))
            s_scr[hh, :, 0:C] = _dot_nt(q, kc_ref[p])
            if windowed:
                s_scr[hh, :, C:] = _dot_nt(q, kx_ref[p, ks, :]) + bias_ref[0, hh]
        accs = []
        for hh in range(H_NA):
            m = jnp.max(s_scr[hh, :, 0:n_k], axis=-1, keepdims=True)
            acc = _dot(jnp.exp2(s_scr[hh, :, 0:C] - m).astype(BF16), vc_ref[hh])
            if windowed:
                acc = acc + _dot(jnp.exp2(s_scr[hh, :, C:] - m).astype(BF16), vx_ref[hh, ks, :])
            accs.append(acc)
        for p in range(H_NA // 2):
            o_ref[:, p * LANE:(p + 1) * LANE] = _pair_out(accs[2 * p], accs[2 * p + 1], lane).astype(o_ref.dtype)

    if not with_ctx:
        step(True)
        return
    is_x = pl.program_id(1) < n_g
    pl.when(is_x)(lambda: step(True))
    pl.when(jnp.logical_not(is_x))(lambda: step(False))


def _na_attention(q, k, v, bias, with_ctx):
    n_g = ROWS // NA_QROWS
    n_steps = n_g + (1 if with_ctx else 0)
    xq = NX // TQ
    qrow = lambda b, g: jnp.where(g < n_g, b * n_g + g, xq + b)
    kind = lambda b, g: (jnp.where(g == 0, 0, jnp.where(g >= n_g - 1, 2, 1)), 0, 0, 0)
    hp = H_NA // 2
    return pl.pallas_call(
        functools.partial(_na_kernel, n_g=n_g, with_ctx=with_ctx),
        grid=(B, n_steps),
        in_specs=[
            pl.BlockSpec((hp, TQ, LANE), lambda b, g: (0, qrow(b, g), 0)),
            pl.BlockSpec((hp, S, LANE), lambda b, g: (0, b, 0)),
            pl.BlockSpec((hp, C, LANE), lambda b, g: (0, NX // C + b, 0)),
            pl.BlockSpec((H_NA, S, LANE), lambda b, g: (0, b, 0)),
            pl.BlockSpec((H_NA, C, LANE), lambda b, g: (0, NX // C + b, 0)),
            pl.BlockSpec((1, H_NA, TQ, NA_KROWS * GRID_W), kind),
        ],
        out_specs=pl.BlockSpec((TQ, NA_W), lambda b, g: (qrow(b, g), 0)),
        out_shape=jax.ShapeDtypeStruct((N if with_ctx else NX, NA_W), BF16),
        scratch_shapes=[pltpu.VMEM((H_NA, TQ, C + NA_KROWS * GRID_W), F32)],
        compiler_params=pltpu.CompilerParams(dimension_semantics=("parallel", "arbitrary"),
                                             vmem_limit_bytes=VMEM_BIG),
        name="na_attention",
    )(q, k, k, v, v, bias)


def _na_bias_table(rpb_l):
    qc = np.arange(GRID_W)
    kc = np.arange(GRID_W)
    ws = np.clip(qc - WIN_C // 2, 0, GRID_W - WIN_C)
    col_ok = (kc[None, :] >= ws[:, None]) & (kc[None, :] < ws[:, None] + WIN_C)
    dc = np.clip(kc[None, :] - qc[:, None] + WIN_C - 1, 0, 2 * WIN_C - 2)
    sel_c = (dc[None] == np.arange(2 * WIN_C - 1)[:, None, None]).astype(np.float32)
    n_g = ROWS // NA_QROWS
    sel_r, oks = [], []
    for g in (0, 1, n_g - 1):
        base = int(np.clip(g * NA_QROWS - WIN_R // 2, 0, ROWS - NA_KROWS))
        r = g * NA_QROWS + np.arange(NA_QROWS)
        rs = np.clip(r - WIN_R // 2, 0, ROWS - WIN_R)
        kr = base + np.arange(NA_KROWS)
        row_ok = (kr[None, :] >= rs[:, None]) & (kr[None, :] < rs[:, None] + WIN_R)
        ri = kr[None, :] - r[:, None] + WIN_R - 1
        sel_r.append(((ri[:, :, None] == np.arange(2 * WIN_R - 1)) & row_ok[:, :, None]).astype(np.float32))
        oks.append(row_ok[:, None, :, None] & col_ok[None, :, None, :])
    hi = lax.Precision.HIGHEST
    by_col = jnp.einsum("hrd,dqk->hrqk", rpb_l * LOG2E, jnp.asarray(sel_c), precision=hi)
    bias = jnp.einsum("glar,hrqk->ghlqak", jnp.asarray(np.stack(sel_r)), by_col, precision=hi)
    bias = jnp.where(jnp.asarray(np.stack(oks))[:, None], bias, NEG)
    return bias.reshape(3, H_NA, TQ, NA_KROWS * GRID_W).astype(F32)


def _merge_core(ym_ref, yn_ref, yc_ref, sg_ref, ha_ref, hb_ref, mod_ref, gpost_ref, gffn_ref, wbr_ref, wout_ref):
    o1, o2 = H_MLA * VD, H_MLA * VD + NA_W
    z = (sg_ref[:, 0:D].astype(F32) * _dot(ym_ref[...], wbr_ref[0:o1, :])
         + sg_ref[:, D:2 * D].astype(F32) * _dot(yn_ref[...], wbr_ref[o1:o2, :])
         + sg_ref[:, 2 * D:3 * D].astype(F32) * _dot(yc_ref[...], wbr_ref[o2:, :]))
    o = _dot(z.astype(BF16), wout_ref[...])
    h1 = _stream_tile(ha_ref, hb_ref) + mod_ref[0, 2:3, :] * _rms(o, gpost_ref[...])
    t = _rms(h1, gffn_ref[...]) * (1.0 + mod_ref[0, 4:5, :]) + mod_ref[0, 3:4, :]
    return h1, t


def _merge_kernel(ym_ref, yn_ref, yc_ref, sg_ref, ha_ref, hb_ref, mod_ref, gpost_ref, gffn_ref, wbr_ref, wout_ref,
                  h1_ref, t_ref):
    h1, t = _merge_core(ym_ref, yn_ref, yc_ref, sg_ref, ha_ref, hb_ref, mod_ref, gpost_ref, gffn_ref, wbr_ref,
                        wout_ref)
    h1_ref[...] = h1
    t_ref[...] = t.astype(BF16)


def _merge_route_kernel(ym_ref, yn_ref, yc_ref, sg_ref, ha_ref, hb_ref, mod_ref, gpost_ref, gffn_ref, wbr_ref,
                        wout_ref, wr_ref, h1_ref, t_ref, e_ref, w_ref, cnt_ref):
    h1, t = _merge_core(ym_ref, yn_ref, yc_ref, sg_ref, ha_ref, hb_ref, mod_ref, gpost_ref, gffn_ref, wbr_ref,
                        wout_ref)
    h1_ref[...] = h1
    t_hi = t.astype(BF16)
    t_ref[...] = t_hi
    t_lo = (t - t_hi.astype(F32)).astype(BF16)
    hi = _dot(t_hi, wr_ref[...])
    logits = hi[:, :LANE] + hi[:, LANE:] + _dot(t_lo, wr_ref[:, :LANE])
    lane = lax.broadcasted_iota(jnp.int32, logits.shape, 1)
    logits = jnp.where(lane < N_EXP, logits, -jnp.inf)
    l1 = jnp.max(logits, axis=-1, keepdims=True)
    e1 = jnp.min(jnp.where(logits == l1, lane, LANE), axis=-1, keepdims=True)
    rest = jnp.where(lane == e1, -jnp.inf, logits)
    l2 = jnp.max(rest, axis=-1, keepdims=True)
    e2 = jnp.min(jnp.where(rest == l2, lane, LANE), axis=-1, keepdims=True)
    ex = jnp.exp(l2 - l1)
    den = 1.0 + ex
    w_ref[...] = jnp.where(lane == 0, 1.0 / den, jnp.where(lane == 1, ex / den, 0.0))

    @pl.when(pl.program_id(0) == 0)
    def _():
        cnt_ref[...] = jnp.zeros(cnt_ref.shape, F32)

    oh1 = lane == e1
    oh2 = lane == e2
    both = jnp.where(oh1 | oh2, 1.0, 0.0)
    r_i = lax.broadcasted_iota(jnp.int32, (TM, TM), 0)
    c_i = lax.broadcasted_iota(jnp.int32, (TM, TM), 1)
    tri = jnp.where(c_i < r_i, 1.0, 0.0).astype(BF16)
    run = cnt_ref[0:1, :]
    before = _dot(tri, both.astype(BF16)) + run
    rank1 = jnp.sum(jnp.where(oh1, before, 0.0), axis=-1, keepdims=True).astype(jnp.int32)
    rank2 = jnp.sum(jnp.where(oh2, before, 0.0), axis=-1, keepdims=True).astype(jnp.int32)
    e_ref[...] = jnp.where(lane == 0, e1, jnp.where(lane == 1, e2, jnp.where(lane == 2, rank1,
                                                                               jnp.where(lane == 3, rank2, 0))))
    cnt_ref[...] = jnp.broadcast_to(run + jnp.sum(both, axis=0, keepdims=True), cnt_ref.shape)


def _merge(ym, yn, yc, sg, h_parts, mod_l, gpost, gffn, wl, n_rows, w_router=None):
    row = lambda i: (i, 0)
    in_specs = [
        pl.BlockSpec((TM, H_MLA * VD), row),
        pl.BlockSpec((TM, NA_W), row),
        pl.BlockSpec((TM, CM_W), row),
        pl.BlockSpec((TM, 3 * D), row),
    ] + _stream_specs(h_parts[0] is not h_parts[1]) + [
        pl.BlockSpec((1, N_MOD, D), _mod_index(S // TM)),
        _const_spec((1, D)),
        _const_spec((1, D)),
        _const_spec((D, D)),
        _const_spec((D, D)),
    ]
    out_shape = [jax.ShapeDtypeStruct((n_rows, D), F32), jax.ShapeDtypeStruct((n_rows, D), BF16)]
    out_specs = [pl.BlockSpec((TM, D), row), pl.BlockSpec((TM, D), row)]
    args = [ym, yn, yc, sg, *h_parts, mod_l, gpost, gffn, wl["w_br"], wl["w_out"]]
    if w_router is None:
        body, name, sem = _merge_kernel, "merge", "parallel"
    else:
        body, name, sem = _merge_route_kernel, "merge_route", "arbitrary"
        in_specs.append(_const_spec((D, 2 * LANE)))
        args.append(w_router)
        out_shape += [jax.ShapeDtypeStruct((n_rows, LANE), jnp.int32), jax.ShapeDtypeStruct((n_rows, LANE), F32),
                      jax.ShapeDtypeStruct((8, LANE), F32)]
        out_specs += [pl.BlockSpec((TM, LANE), row), pl.BlockSpec((TM, LANE), row),
                      pl.BlockSpec((8, LANE), lambda i: (0, 0))]
    return pl.pallas_call(
        body,
        grid=(n_rows // TM,),
        in_specs=in_specs,
        out_specs=out_specs,
        out_shape=out_shape,
        compiler_params=pltpu.CompilerParams(dimension_semantics=(sem,), vmem_limit_bytes=VMEM_BIG),
        name=name,
    )(*args)


def _ffn_kernel(t_ref, h1_ref, mod_ref, g_ref, w1_ref, w3_ref, w2_ref, o_ref):
    t = t_ref[...]
    half = D_FF // 2
    f = None
    for j in range(2):
        sl = slice(j * half, (j + 1) * half)
        a = _dot(t, w1_ref[:, sl])
        b = _dot(t, w3_ref[:, sl])
        part = _dot((a * jax.nn.sigmoid(a) * b).astype(BF16), w2_ref[sl, :])
        f = part if f is None else f + part
    o_ref[...] = h1_ref[...] + mod_ref[0, 5:6, :] * _rms(f, g_ref[...])


def _dense_ffn(t, h1, mod_l, g, w1, w3, w2):
    row = lambda i: (i, 0)
    return pl.pallas_call(
        _ffn_kernel,
        grid=(N // TM,),
        in_specs=[
            pl.BlockSpec((TM, D), row),
            pl.BlockSpec((TM, D), row),
            pl.BlockSpec((1, N_MOD, D), _mod_index(S // TM)),
            _const_spec((1, D)),
            _const_spec((D, D_FF)),
            _const_spec((D, D_FF)),
            _const_spec((D_FF, D)),
        ],
        out_specs=pl.BlockSpec((TM, D), row),
        out_shape=jax.ShapeDtypeStruct((N, D), F32),
        compiler_params=pltpu.CompilerParams(dimension_semantics=("parallel",), vmem_limit_bytes=VMEM_BIG),
        name="dense_ffn",
    )(t, h1, mod_l, g, w1, w3, w2)


def _moe_kernel(be_ref, nu_ref, x_ref, w1_hbm, w3_hbm, w2_hbm, y_ref, wb1, wb3, wb2, st_in, st_out, sem):
    j = pl.program_id(0)
    n_f = D_FFE // FT
    e = be_ref[j]
    fresh = jnp.logical_or(j == 0, e != be_ref[jnp.maximum(j - 1, 0)])

    @pl.when(fresh)
    def _():
        def chunk_copy(k):
            mat, f = divmod(k, n_f)
            slot = k % 2
            cols = pl.ds(f * FT, FT)
            if mat == 2:
                return pltpu.make_async_copy(w2_hbm.at[e, cols, :], st_out.at[slot], sem.at[slot])
            return pltpu.make_async_copy((w1_hbm, w3_hbm)[mat].at[e, :, cols], st_in.at[slot], sem.at[slot])

        chunk_copy(0).start()
        for k in range(3 * n_f):
            if k + 1 < 3 * n_f:
                chunk_copy(k + 1).start()
            chunk_copy(k).wait()
            mat, f = divmod(k, n_f)
            slot = k % 2
            sl = slice(f * FT, (f + 1) * FT)
            if mat == 0:
                wb1[:, sl] = st_in[slot].astype(BF16)
            elif mat == 1:
                wb3[:, sl] = st_in[slot].astype(BF16)
            else:
                wb2[sl, :] = st_out[slot].astype(BF16)

    used = j < nu_ref[0]

    @pl.when(used)
    def _():
        x = x_ref[...]
        for f in range(n_f):
            sl = slice(f * FT, (f + 1) * FT)
            a = _dot(x, wb1[:, sl])
            b = _dot(x, wb3[:, sl])
            part = _dot((a * jax.nn.sigmoid(a) * b).astype(BF16), wb2[sl, :])
            if f == 0:
                y_ref[...] = part
            else:
                y_ref[...] += part

    @pl.when(jnp.logical_not(used))
    def _():
        y_ref[...] = jnp.zeros(y_ref.shape, y_ref.dtype)


def _moe_experts(x_pad, blk_expert, n_used, w1, w3, w2):
    hbm = pl.BlockSpec(memory_space=pl.ANY)
    grid_spec = pltpu.PrefetchScalarGridSpec(
        num_scalar_prefetch=2,
        grid=(N_BLK,),
        in_specs=[pl.BlockSpec((MOE_BLK, D), lambda j, be, nu: (j, 0)), hbm, hbm, hbm],
        out_specs=pl.BlockSpec((MOE_BLK, D), lambda j, be, nu: (j, 0)),
        scratch_shapes=[
            pltpu.VMEM((D, D_FFE), BF16), pltpu.VMEM((D, D_FFE), BF16), pltpu.VMEM((D_FFE, D), BF16),
            pltpu.VMEM((2, D, FT), F32), pltpu.VMEM((2, FT, D), F32), pltpu.SemaphoreType.DMA((2,)),
        ],
    )
    return pl.pallas_call(
        _moe_kernel,
        grid_spec=grid_spec,
        out_shape=jax.ShapeDtypeStruct((N_PAD, D), F32),
        compiler_params=pltpu.CompilerParams(dimension_semantics=("arbitrary",), vmem_limit_bytes=VMEM_BIG),
        name="moe_experts",
    )(blk_expert, n_used, x_pad, w1, w3, w2)


def _combine_kernel(ya_ref, yb_ref, w_ref, h1_ref, mod_ref, g_ref, o_ref):
    f = w_ref[:, 0:1] * ya_ref[...] + w_ref[:, 1:2] * yb_ref[...]
    o_ref[...] = h1_ref[...] + mod_ref[0, 5:6, :] * _rms(f, g_ref[...])


def _moe_combine(ya, yb, w, h1, mod_l, g):
    row = lambda i: (i, 0)
    return pl.pallas_call(
        _combine_kernel,
        grid=(NX // TM,),
        in_specs=[
            pl.BlockSpec((TM, D), row),
            pl.BlockSpec((TM, D), row),
            pl.BlockSpec((TM, LANE), row),
            pl.BlockSpec((TM, D), row),
            pl.BlockSpec((1, N_MOD, D), _mod_index(S // TM)),
            _const_spec((1, D)),
        ],
        out_specs=pl.BlockSpec((TM, D), row),
        out_shape=jax.ShapeDtypeStruct((NX, D), F32),
        compiler_params=pltpu.CompilerParams(dimension_semantics=("parallel",), vmem_limit_bytes=VMEM_BIG),
        name="moe_combine",
    )(ya, yb, w, h1, mod_l, g)


def _moe_layout(e, rank, counts):
    padded = (counts + MOE_BLK - 1) // MOE_BLK * MOE_BLK
    pad_end = jnp.cumsum(padded)
    pad_start = pad_end - padded
    onehot = e[:, :, None] == jnp.arange(N_EXP, dtype=jnp.int32)
    dest = jnp.sum(jnp.where(onehot, pad_start, 0), axis=-1) + rank
    blk_start = jnp.arange(N_BLK, dtype=jnp.int32) * MOE_BLK
    blk_expert = jnp.minimum(jnp.sum((blk_start[:, None] >= pad_end[None, :]).astype(jnp.int32), axis=1), N_EXP - 1)
    n_used = (pad_end[-1] // MOE_BLK).astype(jnp.int32).reshape(1)
    return dest.astype(jnp.int32), blk_expert.astype(jnp.int32), n_used


def _rope_rot(w):
    half, quarter = ROPE // 2, ROPE // 4
    return jnp.concatenate([-w[..., quarter:half], w[..., :quarter], -w[..., half + quarter:], w[..., half:half + quarter]],
                           axis=-1)


def _rope_tables():
    half = ROPE // 2
    inv = ROPE_BASE ** (-(jnp.arange(0, half, 2, dtype=F32) / half))
    t = jnp.arange(S)
    ang_r = (t // GRID_W).astype(F32)[:, None] * inv
    ang_c = (t % GRID_W).astype(F32)[:, None] * inv
    ang = jnp.concatenate([ang_r, ang_r, ang_c, ang_c], axis=-1)
    cos32, sin32 = jnp.cos(ang), jnp.sin(ang)
    cos = jnp.ones((S + TM, LANE), F32).at[:S, NOPE:NOPE + ROPE].set(cos32)
    sin = jnp.zeros((S + TM, LANE), F32).at[:S, NOPE:NOPE + ROPE].set(sin32)
    return cos, sin


def _pad_heads(w, n_heads, width):
    w = w.reshape(w.shape[0], n_heads, width)
    return jnp.concatenate([w, jnp.zeros((w.shape[0], n_heads, LANE - width), w.dtype)], axis=2).reshape(w.shape[0], -1)


def _pack_layer(l, w_in, g_q, w_uq, g_kv, w_ukv, g_v, b_v, w_sp, b_sp, w_br, w_out):
    wi = w_in[l]
    offs = np.cumsum((0, Q_LORA, KV_LORA, ROPE, NA_W, NA_W, NA_W, CM_W, CM_W))
    cq, ckv, kr, naq, nak, nav, u, v = (wi[:, offs[i]:offs[i + 1]] for i in range(8))
    gl = wi[:, offs[8]:]
    zl = jnp.zeros((D, NOPE), F32)
    zr = jnp.zeros((D, LANE - NOPE - ROPE), F32)
    w_in_p = jnp.concatenate([cq, ckv, zl, kr, zr, zl, _rope_rot(kr), zr, naq, nak, _pad_heads(nav, H_NA, HD_NA),
                              u, v, gl], axis=1).astype(BF16)
    uq = w_uq[l].reshape(Q_LORA, H_MLA, QK)
    zq = jnp.zeros((Q_LORA, H_MLA, LANE - QK), F32)
    w_qa = jnp.concatenate([uq, zq], axis=2).reshape(Q_LORA, H_MLA * LANE).astype(BF16)
    w_qb = jnp.concatenate([jnp.zeros((Q_LORA, H_MLA, NOPE), F32), _rope_rot(uq[:, :, NOPE:]), zq], axis=2)
    w_qb = w_qb.reshape(Q_LORA, H_MLA * LANE).astype(BF16)
    ukv = w_ukv[l].reshape(KV_LORA, H_MLA, NOPE + VD)
    w_uk = _pad_heads(ukv[:, :, :NOPE].reshape(KV_LORA, -1), H_MLA, NOPE).astype(BF16)
    w_uv = _pad_heads(ukv[:, :, NOPE:].reshape(KV_LORA, -1), H_MLA, VD).astype(BF16)
    b_sp_t = jnp.repeat(b_sp[l].T, CM_W // CM_G, axis=1)
    return dict(
        w_in=w_in_p, g_q=g_q[l].reshape(1, -1), w_qa=w_qa, w_qb=w_qb, g_kv=g_kv[l].reshape(1, -1), w_uk=w_uk,
        w_uv=w_uv, g_v=g_v[l].reshape(1, -1), b_v=b_v[l].reshape(1, -1), w_sp=w_sp[l].astype(BF16), b_sp=b_sp_t,
        w_br=w_br[l].astype(BF16), w_out=w_out[l].astype(BF16))


def kernel(x, c, ctx, c_ctx, w_mod, b_mod, g_pre_mix, g_post_mix, g_pre_ffn, g_post_ffn, w_in, g_q, w_uq, g_kv, w_ukv, rpb, g_v, b_v, w_sp, b_sp, w_br, w_out, w_ffn1, w_ffn3, w_ffn2, w_router, w_moe1, w_moe3, w_moe2):
    assert DEPTH == 2, "layer 0 is the dense layer with context outputs, layer 1 the last (expert) layer"
    h_parts = (x.reshape(NX, D), ctx.reshape(NC, D))
    c_all = jnp.concatenate([c, c_ctx[None, :], jnp.zeros((16 - B - 1, D), F32)], axis=0)
    mod = _modulation(c_all, w_mod, b_mod).reshape(DEPTH, 16, N_MOD, D)
    rope_cos, rope_sin = _rope_tables()
    row1 = lambda a: a.reshape(1, -1)

    for l in range(DEPTH):
        last = l == DEPTH - 1
        wl = _pack_layer(l, w_in, g_q, w_uq, g_kv, w_ukv, g_v, b_v, w_sp, b_sp, w_br, w_out)
        mod_l = mod[l]
        q, k, v, naq, nak, nav, ycm, sg = _inproj(h_parts, mod_l, row1(g_pre_mix[l]), wl, rope_cos, rope_sin)
        ym = _mla_attention(q, k, v, with_ctx=not last)
        yn = _na_attention(naq, nak, nav, _na_bias_table(rpb[l]), with_ctx=not last)
        i = l // 2
        if not last:
            h1, t = _merge(ym, yn, ycm, sg, h_parts, mod_l, row1(g_post_mix[l]), row1(g_pre_ffn[l]), wl, N)
            h = _dense_ffn(t, h1, mod_l, row1(g_post_ffn[l]), w_ffn1[i].astype(BF16), w_ffn3[i].astype(BF16),
                           w_ffn2[i].astype(BF16))
            h_parts = (h, h)
        else:
            wr_hi = w_router[i].astype(BF16)
            wr_lo = (w_router[i] - wr_hi.astype(F32)).astype(BF16)
            zpad = jnp.zeros((D, LANE - N_EXP), BF16)
            wr = jnp.concatenate([wr_hi, zpad, wr_lo, zpad], axis=1)
            h1, t, e, w, cnt = _merge(ym, yn, ycm, sg, h_parts, mod_l, row1(g_post_mix[l]), row1(g_pre_ffn[l]), wl,
                                      NX, wr)
            dest, blk_expert, n_used = _moe_layout(e[:, 0:2], e[:, 2:4], cnt[0, :N_EXP].astype(jnp.int32))
            inb = dict(mode="promise_in_bounds")
            slot_tok = jnp.zeros((N_PAD,), jnp.int32).at[dest.reshape(N_ASG)].set(
                jnp.arange(N_ASG, dtype=jnp.int32) // 2, unique_indices=True, **inb)
            x_pad = t.at[slot_tok].get(**inb)
            y = _moe_experts(x_pad, blk_expert, n_used, w_moe1[i], w_moe3[i], w_moe2[i])
            h = _moe_combine(y.at[dest[:, 0]].get(**inb), y.at[dest[:, 1]].get(**inb), w, h1, mod_l,
                             row1(g_post_ffn[l]))
    return h[:NX].reshape(B, S, D)
```

```python
import functools

import numpy as np
import jax
import jax.numpy as jnp
from jax import lax
from jax.experimental import pallas as pl
from jax.experimental.pallas import tpu as pltpu

F32 = jnp.float32
BF16 = jnp.bfloat16

D = 1024
B = 8
S = 4096
DEPTH = 2
GRID_W = 64
ROWS = S // GRID_W
C = 256
EPS = 1e-6
NEG = -1e30
LOG2E = 1.4426950408889634
N_MOD = 6

H_MLA = 8
Q_LORA = 384
KV_LORA = 256
NOPE = 64
ROPE = 32
VD = 64
QK = NOPE + ROPE
ROPE_BASE = 10000.0

H_NA = 4
HD_NA = 64
NA_W = H_NA * HD_NA
WIN_R = 8
WIN_C = 16
NA_QROWS = 4
NA_KROWS = 12

CM_G = 4
CM_CHUNK = 128
CM_W = 256

D_FF = 2816
N_EXP = 8
D_FFE = 3584
MOE_BLK = 512

NX = B * S
NC = B * C
N = NX + NC
LANE = 128

TM = 512
TQ = 256
MLA_HPS = 4
FT = 512
N_ASG = NX * 2
N_BLK = N_ASG // MOE_BLK + N_EXP
N_PAD = N_BLK * MOE_BLK

O_CQ, O_CKV, O_KRA, O_KRB, O_NAQ, O_NAK, O_NAV, O_U, O_V, O_GL = 0, 384, 640, 768, 896, 1152, 1408, 1920, 2176, 2432
W_IN_PACKED = O_GL + 3 * D

VMEM_BIG = 56 * 1024 * 1024


def _rms(xf, g):
    return xf * lax.rsqrt(jnp.mean(xf * xf, axis=-1, keepdims=True) + EPS) * g


def _dot(a, b):
    return jnp.dot(a, b, preferred_element_type=F32)


def _dot_nt(a, b):
    return lax.dot_general(a, b, (((1,), (1,)), ((), ())), preferred_element_type=F32)


def _const_spec(shape):
    nd = len(shape)
    return pl.BlockSpec(shape, lambda *_: (0,) * nd, pipeline_mode=pl.Buffered(1))


def _mod_index(tiles_per_batch):
    return lambda i: (jnp.minimum(i // tiles_per_batch, B), 0, 0)


def _stream_specs(split):
    nxt = NX // TM
    off = nxt if split else 0
    return [pl.BlockSpec((TM, D), lambda i: (jnp.minimum(i, nxt - 1), 0)),
            pl.BlockSpec((TM, D), lambda i: (jnp.maximum(i, nxt) - off, 0))]


def _stream_tile(ha_ref, hb_ref):
    return jnp.where(pl.program_id(0) < NX // TM, ha_ref[...], hb_ref[...])


def _mod_kernel(c_ref, w_ref, b_ref, o_ref):
    c = c_ref[...]
    sc = c * jax.nn.sigmoid(c)
    o_ref[0] = jnp.dot(sc, w_ref[0], preferred_element_type=F32, precision=lax.Precision.HIGHEST) + b_ref[0]


def _modulation(c_all, w_mod, b_mod):
    tn = 1536
    return pl.pallas_call(
        _mod_kernel,
        grid=(DEPTH, N_MOD * D // tn),
        in_specs=[
            pl.BlockSpec((16, D), lambda l, j: (0, 0)),
            pl.BlockSpec((1, D, tn), lambda l, j: (l, 0, j)),
            pl.BlockSpec((1, 1, tn), lambda l, j: (l, 0, j)),
        ],
        out_specs=pl.BlockSpec((1, 16, tn), lambda l, j: (l, 0, j)),
        out_shape=jax.ShapeDtypeStruct((DEPTH, 16, N_MOD * D), F32),
        compiler_params=pltpu.CompilerParams(dimension_semantics=("parallel", "parallel"), vmem_limit_bytes=VMEM_BIG),
        name="modulation",
    )(c_all, w_mod, b_mod.reshape(DEPTH, 1, N_MOD * D))


def _inproj_kernel(ha_ref, hb_ref, mod_ref, gpre_ref, win_ref, gq_ref, wqa_ref, wqb_ref, gkv_ref, wuk_ref, wuv_ref,
                   cos_ref, sin_ref, gv_ref, bv_ref, wsp_ref, bsp_ref,
                   q_ref, k_ref, v_ref, naq_ref, nak_ref, nav_ref, ycm_ref, sg_ref):
    n = _rms(_stream_tile(ha_ref, hb_ref), gpre_ref[...]) * (1.0 + mod_ref[0, 1:2, :]) + mod_ref[0, 0:1, :]
    nb = n.astype(BF16)

    def proj(a, b):
        return _dot(nb, win_ref[:, a:b])

    cos = cos_ref[...]
    sin = sin_ref[...]
    ones_hi = jnp.where(lax.broadcasted_iota(jnp.int32, (TM, LANE), 1) >= VD, 1.0, 0.0)

    cqn = _rms(proj(O_CQ, O_CKV), gq_ref[...]).astype(BF16)
    qa = _dot(cqn, wqa_ref[...])
    qb = _dot(cqn, wqb_ref[...])
    for hh in range(H_MLA):
        sl = slice(hh * LANE, (hh + 1) * LANE)
        q_ref[hh] = ((qa[:, sl] * cos + qb[:, sl] * sin) * (QK ** -0.5 * LOG2E)).astype(BF16)
    ckvn = _rms(proj(O_CKV, O_KRA), gkv_ref[...]).astype(BF16)
    krp = proj(O_KRA, O_KRB) * cos + proj(O_KRB, O_NAQ) * sin
    kk = _dot(ckvn, wuk_ref[...])
    for hh in range(H_MLA):
        k_ref[hh] = (kk[:, hh * LANE:(hh + 1) * LANE] + krp).astype(BF16)
    vv = _dot(ckvn, wuv_ref[...])
    for hh in range(H_MLA):
        v_ref[hh] = (vv[:, hh * LANE:(hh + 1) * LANE] + ones_hi).astype(BF16)

    naq = proj(O_NAQ, O_NAK) * (HD_NA ** -0.5 * LOG2E)
    nak = proj(O_NAK, O_NAV)
    for p in range(H_NA // 2):
        sl = slice(p * LANE, (p + 1) * LANE)
        naq_ref[p] = naq[:, sl].astype(BF16)
        nak_ref[p] = nak[:, sl].astype(BF16)
    nav = proj(O_NAV, O_U)
    for hh in range(H_NA):
        nav_ref[hh] = (nav[:, hh * LANE:(hh + 1) * LANE] + ones_hi).astype(BF16)

    u = proj(O_U, O_V)
    v = proj(O_V, O_GL)
    mu = jnp.mean(v, axis=-1, keepdims=True)
    vc = v - mu
    vn = (vc * lax.rsqrt(jnp.mean(vc * vc, axis=-1, keepdims=True) + EPS) * gv_ref[...] + bv_ref[...]).astype(BF16)
    grp = lax.broadcasted_iota(jnp.int32, (CM_CHUNK, CM_W), 1) // (CM_W // CM_G)
    for ch in range(TM // CM_CHUNK):
        rs = slice(ch * CM_CHUNK, (ch + 1) * CM_CHUNK)
        vch = vn[rs, :]
        sp = _dot(wsp_ref[CM_G - 1], vch)
        for g in range(CM_G - 2, -1, -1):
            sp = jnp.where(grp == g, _dot(wsp_ref[g], vch), sp)
        ycm_ref[rs, :] = (u[rs, :] * (sp + bsp_ref[...])).astype(BF16)

    for j in range(3):
        sg_ref[:, j * D:(j + 1) * D] = jax.nn.sigmoid(proj(O_GL + j * D, O_GL + (j + 1) * D)).astype(BF16)


def _inproj(h_parts, mod_l, gpre, wl, rope_cos, rope_sin):
    tiles_x = S // TM
    n_tiles = N // TM
    pos_idx = lambda i: (jnp.where(i < NX // TM, i % tiles_x, tiles_x), 0)
    row = lambda i: (i, 0)
    hrow = lambda i: (0, i, 0)
    in_specs = _stream_specs(h_parts[0] is not h_parts[1]) + [
        pl.BlockSpec((1, N_MOD, D), _mod_index(tiles_x)),
        _const_spec((1, D)),
        _const_spec((D, W_IN_PACKED)),
        _const_spec((1, Q_LORA)),
        _const_spec((Q_LORA, H_MLA * LANE)),
        _const_spec((Q_LORA, H_MLA * LANE)),
        _const_spec((1, KV_LORA)),
        _const_spec((KV_LORA, H_MLA * LANE)),
        _const_spec((KV_LORA, H_MLA * LANE)),
        pl.BlockSpec((TM, LANE), pos_idx),
        pl.BlockSpec((TM, LANE), pos_idx),
        _const_spec((1, CM_W)),
        _const_spec((1, CM_W)),
        _const_spec((CM_G, CM_CHUNK, CM_CHUNK)),
        _const_spec((CM_CHUNK, CM_W)),
    ]
    heads = lambda n: (jax.ShapeDtypeStruct((n, N, LANE), BF16), pl.BlockSpec((n, TM, LANE), hrow))
    outs = [
        heads(H_MLA), heads(H_MLA), heads(H_MLA), heads(H_NA // 2), heads(H_NA // 2), heads(H_NA),
        (jax.ShapeDtypeStruct((N, CM_W), BF16), pl.BlockSpec((TM, CM_W), row)),
        (jax.ShapeDtypeStruct((N, 3 * D), BF16), pl.BlockSpec((TM, 3 * D), row)),
    ]
    return pl.pallas_call(
        _inproj_kernel,
        grid=(n_tiles,),
        in_specs=in_specs,
        out_specs=[o[1] for o in outs],
        out_shape=[o[0] for o in outs],
        compiler_params=pltpu.CompilerParams(dimension_semantics=("parallel",), vmem_limit_bytes=VMEM_BIG),
        name="inproj",
    )(*h_parts, mod_l, gpre, wl["w_in"], wl["g_q"], wl["w_qa"], wl["w_qb"], wl["g_kv"], wl["w_uk"], wl["w_uv"],
      rope_cos, rope_sin, wl["g_v"], wl["b_v"], wl["w_sp"], wl["b_sp"])


def _pair_out(acc0, acc1, lane):
    return jnp.where(lane < VD, acc0 / pltpu.roll(acc0, VD, 1), pltpu.roll(acc1, VD, 1) / acc1)


def _mla_kernel(q_ref, kx_ref, kc_ref, vx_ref, vc_ref, o_ref, s_scr, *, n_qx, with_ctx):
    lane = lax.broadcasted_iota(jnp.int32, (TQ, LANE), 1)

    def step(with_x):
        n_k = C + (S if with_x else 0)
        for hh in range(MLA_HPS):
            q = q_ref[hh]
            s_scr[hh, :, 0:C] = _dot_nt(q, kc_ref[hh])
            if with_x:
                s_scr[hh, :, C:] = _dot_nt(q, kx_ref[hh])
        accs = []
        for hh in range(MLA_HPS):
            m = jnp.max(s_scr[hh, :, 0:n_k], axis=-1, keepdims=True)
            acc = _dot(jnp.exp2(s_scr[hh, :, 0:C] - m).astype(BF16), vc_ref[hh])
            if with_x:
                acc = acc + _dot(jnp.exp2(s_scr[hh, :, C:] - m).astype(BF16), vx_ref[hh])
            accs.append(acc)
        for p in range(MLA_HPS // 2):
            o_ref[:, p * LANE:(p + 1) * LANE] = _pair_out(accs[2 * p], accs[2 * p + 1], lane).astype(o_ref.dtype)

    if not with_ctx:
        step(True)
        return
    is_x = pl.program_id(2) < n_qx
    pl.when(is_x)(lambda: step(True))
    pl.when(jnp.logical_not(is_x))(lambda: step(False))


def _mla_attention(q, k, v, with_ctx):
    n_qx = S // TQ
    n_q = n_qx + (1 if with_ctx else 0)
    xq = NX // TQ
    qrow = lambda b, q_: jnp.where(q_ < n_qx, b * n_qx + q_, xq + b)
    hps = MLA_HPS
    xkeys = pl.BlockSpec((hps, S, LANE), lambda b, p, q_: (p, b, 0))
    ckeys = pl.BlockSpec((hps, C, LANE), lambda b, p, q_: (p, NX // C + b, 0))
    return pl.pallas_call(
        functools.partial(_mla_kernel, n_qx=n_qx, with_ctx=with_ctx),
        grid=(B, H_MLA // hps, n_q),
        in_specs=[pl.BlockSpec((hps, TQ, LANE), lambda b, p, q_: (p, qrow(b, q_), 0)), xkeys, ckeys, xkeys, ckeys],
        out_specs=pl.BlockSpec((TQ, hps * VD), lambda b, p, q_: (qrow(b, q_), p)),
        out_shape=jax.ShapeDtypeStruct((N if with_ctx else NX, H_MLA * VD), BF16),
        scratch_shapes=[pltpu.VMEM((hps, TQ, C + S), F32)],
        compiler_params=pltpu.CompilerParams(dimension_semantics=("parallel", "parallel", "arbitrary"),
                                             vmem_limit_bytes=VMEM_BIG),
        name="mla_attention",
    )(q, k, k, v, v)


def _na_kernel(q_ref, kx_ref, kc_ref, vx_ref, vc_ref, bias_ref, o_ref, s_scr, *, n_g, with_ctx):
    lane = lax.broadcasted_iota(jnp.int32, (TQ, LANE), 1)

    def step(windowed):
        n_w = NA_KROWS * GRID_W
        n_k = C + (n_w if windowed else 0)
        if windowed:
            base = jnp.clip(pl.program_id(1) * NA_QROWS - WIN_R // 2, 0, ROWS - NA_KROWS)
            ks = pl.ds(pl.multiple_of(base * GRID_W, GRID_W), n_w)
        for hh in range(H_NA):
            p, half = divmod(hh, 2)
            qp = q_ref[p]
            q = jnp.where((lane < HD_NA) == (half == 0), qp, jnp.zeros_like(qp))
            s_scr[hh, :, 0:C] = _dot_nt(q, kc_ref[p])
            if windowed:
                s_scr[hh, :, C:] = _dot_nt(q, kx_ref[p, ks, :]) + bias_ref[0, hh]
        accs = []
        for hh in range(H_NA):
            m = jnp.max(s_scr[hh, :, 0:n_k], axis=-1, keepdims=True)
            acc = _dot(jnp.exp2(s_scr[hh, :, 0:C] - m).astype(BF16), vc_ref[hh])
            if windowed:
                acc = acc + _dot(jnp.exp2(s_scr[hh, :, C:] - m).astype(BF16), vx_ref[hh, ks, :])
            accs.append(acc)
        for p in range(H_NA // 2):
            o_ref[:, p * LANE:(p + 1) * LANE] = _pair_out(accs[2 * p], accs[2 * p + 1], lane).astype(o_ref.dtype)

    if not with_ctx:
        step(True)
        return
    is_x = pl.program_id(1) < n_g
    pl.when(is_x)(lambda: step(True))
    pl.when(jnp.logical_not(is_x))(lambda: step(False))


def _na_attention(q, k, v, bias, with_ctx):
    n_g = ROWS // NA_QROWS
    n_steps = n_g + (1 if with_ctx else 0)
    xq = NX // TQ
    qrow = lambda b, g: jnp.where(g < n_g, b * n_g + g, xq + b)
    kind = lambda b, g: (jnp.where(g == 0, 0, jnp.where(g >= n_g - 1, 2, 1)), 0, 0, 0)
    hp = H_NA // 2
    return pl.pallas_call(
        functools.partial(_na_kernel, n_g=n_g, with_ctx=with_ctx),
        grid=(B, n_steps),
        in_specs=[
            pl.BlockSpec((hp, TQ, LANE), lambda b, g: (0, qrow(b, g), 0)),
            pl.BlockSpec((hp, S, LANE), lambda b, g: (0, b, 0)),
            pl.BlockSpec((hp, C, LANE), lambda b, g: (0, NX // C + b, 0)),
            pl.BlockSpec((H_NA, S, LANE), lambda b, g: (0, b, 0)),
            pl.BlockSpec((H_NA, C, LANE), lambda b, g: (0, NX // C + b, 0)),
            pl.BlockSpec((1, H_NA, TQ, NA_KROWS * GRID_W), kind),
        ],
        out_specs=pl.BlockSpec((TQ, NA_W), lambda b, g: (qrow(b, g), 0)),
        out_shape=jax.ShapeDtypeStruct((N if with_ctx else NX, NA_W), BF16),
        scratch_shapes=[pltpu.VMEM((H_NA, TQ, C + NA_KROWS * GRID_W), F32)],
        compiler_params=pltpu.CompilerParams(dimension_semantics=("parallel", "arbitrary"),
                                             vmem_limit_bytes=VMEM_BIG),
        name="na_attention",
    )(q, k, k, v, v, bias)


def _na_bias_table(rpb_l):
    qc = np.arange(GRID_W)
    kc = np.arange(GRID_W)
    ws = np.clip(qc - WIN_C // 2, 0, GRID_W - WIN_C)
    col_ok = (kc[None, :] >= ws[:, None]) & (kc[None, :] < ws[:, None] + WIN_C)
    dc = np.clip(kc[None, :] - qc[:, None] + WIN_C - 1, 0, 2 * WIN_C - 2)
    sel_c = (dc[None] == np.arange(2 * WIN_C - 1)[:, None, None]).astype(np.float32)
    n_g = ROWS // NA_QROWS
    sel_r, oks = [], []
    for g in (0, 1, n_g - 1):
        base = int(np.clip(g * NA_QROWS - WIN_R // 2, 0, ROWS - NA_KROWS))
        r = g * NA_QROWS + np.arange(NA_QROWS)
        rs = np.clip(r - WIN_R // 2, 0, ROWS - WIN_R)
        kr = base + np.arange(NA_KROWS)
        row_ok = (kr[None, :] >= rs[:, None]) & (kr[None, :] < rs[:, None] + WIN_R)
        ri = kr[None, :] - r[:, None] + WIN_R - 1
        sel_r.append(((ri[:, :, None] == np.arange(2 * WIN_R - 1)) & row_ok[:, :, None]).astype(np.float32))
        oks.append(row_ok[:, None, :, None] & col_ok[None, :, None, :])
    hi = lax.Precision.HIGHEST
    by_col = jnp.einsum("hrd,dqk->hrqk", rpb_l * LOG2E, jnp.asarray(sel_c), precision=hi)
    bias = jnp.einsum("glar,hrqk->ghlqak", jnp.asarray(np.stack(sel_r)), by_col, precision=hi)
    bias = jnp.where(jnp.asarray(np.stack(oks))[:, None], bias, NEG)
    return bias.reshape(3, H_NA, TQ, NA_KROWS * GRID_W).astype(F32)


def _merge_core(ym_ref, yn_ref, yc_ref, sg_ref, ha_ref, hb_ref, mod_ref, gpost_ref, gffn_ref, wbr_ref, wout_ref):
    o1, o2 = H_MLA * VD, H_MLA * VD + NA_W
    z = (sg_ref[:, 0:D].astype(F32) * _dot(ym_ref[...], wbr_ref[0:o1, :])
         + sg_ref[:, D:2 * D].astype(F32) * _dot(yn_ref[...], wbr_ref[o1:o2, :])
         + sg_ref[:, 2 * D:3 * D].astype(F32) * _dot(yc_ref[...], wbr_ref[o2:, :]))
    o = _dot(z.astype(BF16), wout_ref[...])
    h1 = _stream_tile(ha_ref, hb_ref) + mod_ref[0, 2:3, :] * _rms(o, gpost_ref[...])
    t = _rms(h1, gffn_ref[...]) * (1.0 + mod_ref[0, 4:5, :]) + mod_ref[0, 3:4, :]
    return h1, t


def _merge_ffn_kernel(ym_ref, yn_ref, yc_ref, sg_ref, ha_ref, hb_ref, mod_ref, gpost_ref, gffn_ref, wbr_ref,
                      wout_ref, gpf_ref, w1_ref, w3_ref, w2_ref, o_ref):
    h1, t = _merge_core(ym_ref, yn_ref, yc_ref, sg_ref, ha_ref, hb_ref, mod_ref, gpost_ref, gffn_ref, wbr_ref,
                        wout_ref)
    tb = t.astype(BF16)
    half = D_FF // 2
    f = None
    for j in range(2):
        sl = slice(j * half, (j + 1) * half)
        a = _dot(tb, w1_ref[:, sl])
        b = _dot(tb, w3_ref[:, sl])
        part = _dot((a * jax.nn.sigmoid(a) * b).astype(BF16), w2_ref[sl, :])
        f = part if f is None else f + part
    o_ref[...] = h1 + mod_ref[0, 5:6, :] * _rms(f, gpf_ref[...])


def _merge_route_kernel(ym_ref, yn_ref, yc_ref, sg_ref, ha_ref, hb_ref, mod_ref, gpost_ref, gffn_ref, wbr_ref,
                        wout_ref, wr_ref, h1_ref, t_ref, e_ref, w_ref, cnt_ref):
    h1, t = _merge_core(ym_ref, yn_ref, yc_ref, sg_ref, ha_ref, hb_ref, mod_ref, gpost_ref, gffn_ref, wbr_ref,
                        wout_ref)
    h1_ref[...] = h1
    t_ref[...] = t
    t_hi = t.astype(BF16)
    t_lo = (t - t_hi.astype(F32)).astype(BF16)
    hi = _dot(t_hi, wr_ref[...])
    logits = hi[:, :LANE] + hi[:, LANE:] + _dot(t_lo, wr_ref[:, :LANE])
    lane = lax.broadcasted_iota(jnp.int32, logits.shape, 1)
    logits = jnp.where(lane < N_EXP, logits, -jnp.inf)
    l1 = jnp.max(logits, axis=-1, keepdims=True)
    e1 = jnp.min(jnp.where(logits == l1, lane, LANE), axis=-1, keepdims=True)
    rest = jnp.where(lane == e1, -jnp.inf, logits)
    l2 = jnp.max(rest, axis=-1, keepdims=True)
    e2 = jnp.min(jnp.where(rest == l2, lane, LANE), axis=-1, keepdims=True)
    ex = jnp.exp(l2 - l1)
    den = 1.0 + ex
    w_ref[...] = jnp.where(lane == 0, 1.0 / den, jnp.where(lane == 1, ex / den, 0.0))

    @pl.when(pl.program_id(0) == 0)
    def _():
        cnt_ref[...] = jnp.zeros(cnt_ref.shape, F32)

    oh1 = lane == e1
    oh2 = lane == e2
    both = jnp.where(oh1 | oh2, 1.0, 0.0)
    r_i = lax.broadcasted_iota(jnp.int32, (TM, TM), 0)
    c_i = lax.broadcasted_iota(jnp.int32, (TM, TM), 1)
    tri = jnp.where(c_i < r_i, 1.0, 0.0).astype(BF16)
    run = cnt_ref[0:1, :]
    before = _dot(tri, both.astype(BF16)) + run
    rank1 = jnp.sum(jnp.where(oh1, before, 0.0), axis=-1, keepdims=True).astype(jnp.int32)
    rank2 = jnp.sum(jnp.where(oh2, before, 0.0), axis=-1, keepdims=True).astype(jnp.int32)
    e_ref[...] = jnp.where(lane == 0, e1, jnp.where(lane == 1, e2, jnp.where(lane == 2, rank1,
                                                                               jnp.where(lane == 3, rank2, 0))))
    cnt_ref[...] = jnp.broadcast_to(run + jnp.sum(both, axis=0, keepdims=True), cnt_ref.shape)


def _merge_specs(h_parts):
    row = lambda i: (i, 0)
    return [
        pl.BlockSpec((TM, H_MLA * VD), row),
        pl.BlockSpec((TM, NA_W), row),
        pl.BlockSpec((TM, CM_W), row),
        pl.BlockSpec((TM, 3 * D), row),
    ] + _stream_specs(h_parts[0] is not h_parts[1]) + [
        pl.BlockSpec((1, N_MOD, D), _mod_index(S // TM)),
        _const_spec((1, D)),
        _const_spec((1, D)),
        _const_spec((D, D)),
        _const_spec((D, D)),
    ]


def _merge_ffn(ym, yn, yc, sg, h_parts, mod_l, gpost, gffn, wl, gpf, w1, w3, w2):
    row = lambda i: (i, 0)
    return pl.pallas_call(
        _merge_ffn_kernel,
        grid=(N // TM,),
        in_specs=_merge_specs(h_parts) + [_const_spec((1, D)), _const_spec((D, D_FF)), _const_spec((D, D_FF)),
                                          _const_spec((D_FF, D))],
        out_specs=pl.BlockSpec((TM, D), row),
        out_shape=jax.ShapeDtypeStruct((N, D), F32),
        compiler_params=pltpu.CompilerParams(dimension_semantics=("parallel",), vmem_limit_bytes=VMEM_BIG),
        name="merge_ffn",
    )(ym, yn, yc, sg, *h_parts, mod_l, gpost, gffn, wl["w_br"], wl["w_out"], gpf, w1, w3, w2)


def _merge_route(ym, yn, yc, sg, h_parts, mod_l, gpost, gffn, wl, w_router):
    row = lambda i: (i, 0)
    rows = lambda width, dtype: (jax.ShapeDtypeStruct((NX, width), dtype), pl.BlockSpec((TM, width), row))
    outs = [rows(D, F32), rows(D, F32), rows(LANE, jnp.int32), rows(LANE, F32),
            (jax.ShapeDtypeStruct((8, LANE), F32), pl.BlockSpec((8, LANE), lambda i: (0, 0)))]
    return pl.pallas_call(
        _merge_route_kernel,
        grid=(NX // TM,),
        in_specs=_merge_specs(h_parts) + [_const_spec((D, 2 * LANE))],
        out_specs=[o[1] for o in outs],
        out_shape=[o[0] for o in outs],
        compiler_params=pltpu.CompilerParams(dimension_semantics=("arbitrary",), vmem_limit_bytes=VMEM_BIG),
        name="merge_route",
    )(ym, yn, yc, sg, *h_parts, mod_l, gpost, gffn, wl["w_br"], wl["w_out"], w_router)


def _moe_kernel(be_ref, nu_ref, x_ref, w1_hbm, w3_hbm, w2_hbm, y_ref, wb1, wb3, wb2, st_in, st_out, sem):
    j = pl.program_id(0)
    n_f = D_FFE // FT
    e = be_ref[j]
    fresh = jnp.logical_or(j == 0, e != be_ref[jnp.maximum(j - 1, 0)])

    @pl.when(fresh)
    def _():
        def chunk_copy(k):
            mat, f = divmod(k, n_f)
            slot = k % 2
            cols = pl.ds(f * FT, FT)
            if mat == 2:
                return pltpu.make_async_copy(w2_hbm.at[e, cols, :], st_out.at[slot], sem.at[slot])
            return pltpu.make_async_copy((w1_hbm, w3_hbm)[mat].at[e, :, cols], st_in.at[slot], sem.at[slot])

        chunk_copy(0).start()
        for k in range(3 * n_f):
            if k + 1 < 3 * n_f:
                chunk_copy(k + 1).start()
            chunk_copy(k).wait()
            mat, f = divmod(k, n_f)
            slot = k % 2
            sl = slice(f * FT, (f + 1) * FT)
            if mat == 0:
                wb1[:, sl] = st_in[slot].astype(BF16)
            elif mat == 1:
                wb3[:, sl] = st_in[slot].astype(BF16)
            else:
                wb2[sl, :] = st_out[slot].astype(BF16)

    used = j < nu_ref[0]

    @pl.when(used)
    def _():
        x = x_ref[...].astype(BF16)
        for f in range(n_f):
            sl = slice(f * FT, (f + 1) * FT)
            a = _dot(x, wb1[:, sl])
            b = _dot(x, wb3[:, sl])
            part = _dot((a * jax.nn.sigmoid(a) * b).astype(BF16), wb2[sl, :])
            if f == 0:
                y_ref[...] = part
            else:
                y_ref[...] += part

    @pl.when(jnp.logical_not(used))
    def _():
        y_ref[...] = jnp.zeros(y_ref.shape, y_ref.dtype)


def _moe_experts(x_pad, blk_expert, n_used, w1, w3, w2):
    hbm = pl.BlockSpec(memory_space=pl.ANY)
    grid_spec = pltpu.PrefetchScalarGridSpec(
        num_scalar_prefetch=2,
        grid=(N_BLK,),
        in_specs=[pl.BlockSpec((MOE_BLK, D), lambda j, be, nu: (j, 0)), hbm, hbm, hbm],
        out_specs=pl.BlockSpec((MOE_BLK, D), lambda j, be, nu: (j, 0)),
        scratch_shapes=[
            pltpu.VMEM((D, D_FFE), BF16), pltpu.VMEM((D, D_FFE), BF16), pltpu.VMEM((D_FFE, D), BF16),
            pltpu.VMEM((2, D, FT), F32), pltpu.VMEM((2, FT, D), F32), pltpu.SemaphoreType.DMA((2,)),
        ],
    )
    return pl.pallas_call(
        _moe_kernel,
        grid_spec=grid_spec,
        out_shape=jax.ShapeDtypeStruct((N_PAD, D), F32),
        compiler_params=pltpu.CompilerParams(dimension_semantics=("arbitrary",), vmem_limit_bytes=VMEM_BIG),
        name="moe_experts",
    )(blk_expert, n_used, x_pad, w1, w3, w2)


def _combine_kernel(ya_ref, yb_ref, w_ref, h1_ref, mod_ref, g_ref, o_ref):
    f = w_ref[:, 0:1] * ya_ref[...] + w_ref[:, 1:2] * yb_ref[...]
    o_ref[...] = h1_ref[...] + mod_ref[0, 5:6, :] * _rms(f, g_ref[...])


def _moe_combine(ya, yb, w, h1, mod_l, g):
    row = lambda i: (i, 0)
    return pl.pallas_call(
        _combine_kernel,
        grid=(NX // TM,),
        in_specs=[
            pl.BlockSpec((TM, D), row),
            pl.BlockSpec((TM, D), row),
            pl.BlockSpec((TM, LANE), row),
            pl.BlockSpec((TM, D), row),
            pl.BlockSpec((1, N_MOD, D), _mod_index(S // TM)),
            _const_spec((1, D)),
        ],
        out_specs=pl.BlockSpec((TM, D), row),
        out_shape=jax.ShapeDtypeStruct((NX, D), F32),
        compiler_params=pltpu.CompilerParams(dimension_semantics=("parallel",), vmem_limit_bytes=VMEM_BIG),
        name="moe_combine",
    )(ya, yb, w, h1, mod_l, g)


def _moe_layout(e, rank, counts):
    padded = (counts + MOE_BLK - 1) // MOE_BLK * MOE_BLK
    pad_end = jnp.cumsum(padded)
    pad_start = pad_end - padded
    onehot = e[:, :, None] == jnp.arange(N_EXP, dtype=jnp.int32)
    dest = jnp.sum(jnp.where(onehot, pad_start, 0), axis=-1) + rank
    blk_start = jnp.arange(N_BLK, dtype=jnp.int32) * MOE_BLK
    blk_expert = jnp.minimum(jnp.sum((blk_start[:, None] >= pad_end[None, :]).astype(jnp.int32), axis=1), N_EXP - 1)
    n_used = (pad_end[-1] // MOE_BLK).astype(jnp.int32).reshape(1)
    _, tok_sorted = lax.sort_key_val(dest.reshape(N_ASG), jnp.arange(N_ASG, dtype=jnp.int32) // 2)
    squeeze = pad_start - (jnp.cumsum(counts) - counts)
    slot = jnp.arange(N_PAD, dtype=jnp.int32)
    compact = slot - jnp.repeat(squeeze[blk_expert], MOE_BLK)
    slot_tok = tok_sorted.at[jnp.clip(compact, 0, N_ASG - 1)].get(mode="promise_in_bounds")
    return dest.astype(jnp.int32), slot_tok, blk_expert.astype(jnp.int32), n_used


def _rope_rot(w):
    half, quarter = ROPE // 2, ROPE // 4
    return jnp.concatenate([-w[..., quarter:half], w[..., :quarter], -w[..., half + quarter:], w[..., half:half + quarter]],
                           axis=-1)


def _rope_tables():
    half = ROPE // 2
    inv = ROPE_BASE ** (-(jnp.arange(0, half, 2, dtype=F32) / half))
    t = jnp.arange(S)
    ang_r = (t // GRID_W).astype(F32)[:, None] * inv
    ang_c = (t % GRID_W).astype(F32)[:, None] * inv
    ang = jnp.concatenate([ang_r, ang_r, ang_c, ang_c], axis=-1)
    cos32, sin32 = jnp.cos(ang), jnp.sin(ang)
    ones = lambda n: jnp.ones((S, n), F32)
    zeros = lambda n: jnp.zeros((S, n), F32)
    cos = jnp.concatenate([ones(NOPE), cos32, ones(LANE - QK)], axis=1)
    sin = jnp.concatenate([zeros(NOPE), sin32, zeros(LANE - QK)], axis=1)
    return (jnp.concatenate([cos, jnp.ones((TM, LANE), F32)], axis=0),
            jnp.concatenate([sin, jnp.zeros((TM, LANE), F32)], axis=0))


def _pad_heads(w, n_heads, width):
    w = w.reshape(w.shape[0], n_heads, width)
    return jnp.concatenate([w, jnp.zeros((w.shape[0], n_heads, LANE - width), w.dtype)], axis=2).reshape(w.shape[0], -1)


def _pack_layer(l, w_in, g_q, w_uq, g_kv, w_ukv, g_v, b_v, w_sp, b_sp, w_br, w_out):
    wi = w_in[l]
    offs = np.cumsum((0, Q_LORA, KV_LORA, ROPE, NA_W, NA_W, NA_W, CM_W, CM_W))
    cq, ckv, kr, naq, nak, nav, u, v = (wi[:, offs[i]:offs[i + 1]] for i in range(8))
    gl = wi[:, offs[8]:]
    zl = jnp.zeros((D, NOPE), F32)
    zr = jnp.zeros((D, LANE - NOPE - ROPE), F32)
    w_in_p = jnp.concatenate([cq, ckv, zl, kr, zr, zl, _rope_rot(kr), zr, naq, nak, _pad_heads(nav, H_NA, HD_NA),
                              u, v, gl], axis=1).astype(BF16)
    uq = w_uq[l].reshape(Q_LORA, H_MLA, QK)
    zq = jnp.zeros((Q_LORA, H_MLA, LANE - QK), F32)
    w_qa = jnp.concatenate([uq, zq], axis=2).reshape(Q_LORA, H_MLA * LANE).astype(BF16)
    w_qb = jnp.concatenate([jnp.zeros((Q_LORA, H_MLA, NOPE), F32), _rope_rot(uq[:, :, NOPE:]), zq], axis=2)
    w_qb = w_qb.reshape(Q_LORA, H_MLA * LANE).astype(BF16)
    ukv = w_ukv[l].reshape(KV_LORA, H_MLA, NOPE + VD)
    w_uk = _pad_heads(ukv[:, :, :NOPE].reshape(KV_LORA, -1), H_MLA, NOPE).astype(BF16)
    w_uv = _pad_heads(ukv[:, :, NOPE:].reshape(KV_LORA, -1), H_MLA, VD).astype(BF16)
    b_sp_t = jnp.repeat(b_sp[l].T, CM_W // CM_G, axis=1)
    return dict(
        w_in=w_in_p, g_q=g_q[l].reshape(1, -1), w_qa=w_qa, w_qb=w_qb, g_kv=g_kv[l].reshape(1, -1), w_uk=w_uk,
        w_uv=w_uv, g_v=g_v[l].reshape(1, -1), b_v=b_v[l].reshape(1, -1), w_sp=w_sp[l].astype(BF16), b_sp=b_sp_t,
        w_br=w_br[l].astype(BF16), w_out=w_out[l].astype(BF16))


def kernel(x, c, ctx, c_ctx, w_mod, b_mod, g_pre_mix, g_post_mix, g_pre_ffn, g_post_ffn, w_in, g_q, w_uq, g_kv, w_ukv, rpb, g_v, b_v, w_sp, b_sp, w_br, w_out, w_ffn1, w_ffn3, w_ffn2, w_router, w_moe1, w_moe3, w_moe2):
    assert DEPTH == 2, "layer 0 is the dense layer with context outputs, layer 1 the last (expert) layer"
    h_parts = (x.reshape(NX, D), ctx.reshape(NC, D))
    c_all = jnp.concatenate([c, c_ctx[None, :], jnp.zeros((16 - B - 1, D), F32)], axis=0)
    mod = _modulation(c_all, w_mod, b_mod).reshape(DEPTH, 16, N_MOD, D)
    rope_cos, rope_sin = _rope_tables()
    row1 = lambda a: a.reshape(1, -1)

    for l in range(DEPTH):
        last = l == DEPTH - 1
        wl = _pack_layer(l, w_in, g_q, w_uq, g_kv, w_ukv, g_v, b_v, w_sp, b_sp, w_br, w_out)
        mod_l = mod[l]
        q, k, v, naq, nak, nav, ycm, sg = _inproj(h_parts, mod_l, row1(g_pre_mix[l]), wl, rope_cos, rope_sin)
        ym = _mla_attention(q, k, v, with_ctx=not last)
        yn = _na_attention(naq, nak, nav, _na_bias_table(rpb[l]), with_ctx=not last)
        i = l // 2
        if not last:
            h = _merge_ffn(ym, yn, ycm, sg, h_parts, mod_l, row1(g_post_mix[l]), row1(g_pre_ffn[l]), wl,
                           row1(g_post_ffn[l]), w_ffn1[i].astype(BF16), w_ffn3[i].astype(BF16),
                           w_ffn2[i].astype(BF16))
            h_parts = (h, h)
        else:
            wr_hi = w_router[i].astype(BF16)
            wr_lo = (w_router[i] - wr_hi.astype(F32)).astype(BF16)
            zpad = jnp.zeros((D, LANE - N_EXP), BF16)
            wr = jnp.concatenate([wr_hi, zpad, wr_lo, zpad], axis=1)
            h1, t, e, w, cnt = _merge_route(ym, yn, ycm, sg, h_parts, mod_l, row1(g_post_mix[l]),
                                            row1(g_pre_ffn[l]), wl, wr)
            dest, slot_tok, blk_expert, n_used = _moe_layout(e[:, 0:2], e[:, 2:4], cnt[0, :N_EXP].astype(jnp.int32))
            inb = dict(mode="promise_in_bounds")
            x_pad = t.at[slot_tok].get(**inb)
            y = _moe_experts(x_pad, blk_expert, n_used, w_moe1[i], w_moe3[i], w_moe2[i])
            h = _moe_combine(y.at[dest[:, 0]].get(**inb), y.at[dest[:, 1]].get(**inb), w, h1, mod_l,
                             row1(g_post_ffn[l]))
    return h[:NX].reshape(B, S, D)
```

```python
import functools

import numpy as np
import jax
import jax.numpy as jnp
from jax import lax
from jax.experimental import pallas as pl
from jax.experimental.pallas import tpu as pltpu

F32 = jnp.float32
BF16 = jnp.bfloat16

D = 1024
B = 8
S = 4096
DEPTH = 2
GRID_W = 64
ROWS = S // GRID_W
C = 256
EPS = 1e-6
NEG = -1e30
LOG2E = 1.4426950408889634
N_MOD = 6

H_MLA = 8
Q_LORA = 384
KV_LORA = 256
NOPE = 64
ROPE = 32
VD = 64
QK = NOPE + ROPE
ROPE_BASE = 10000.0

H_NA = 4
HD_NA = 64
NA_W = H_NA * HD_NA
WIN_R = 8
WIN_C = 16
NA_QROWS = 4
NA_KROWS = 12

CM_G = 4
CM_CHUNK = 128
CM_W = 256

D_FF = 2816
N_EXP = 8
D_FFE = 3584
MOE_BLK = 512

NX = B * S
NC = B * C
N = NX + NC
LANE = 128

TM = 512
TQ = 256
MLA_HPS = 4
FT = 512
N_ASG = NX * 2
N_BLK = N_ASG // MOE_BLK + N_EXP
N_PAD = N_BLK * MOE_BLK
MOE_PARTS = 4

O_CQ, O_CKV, O_KRA, O_KRB, O_NAQ, O_NAK, O_NAV, O_U, O_V, O_GL = 0, 384, 640, 768, 896, 1152, 1408, 1920, 2176, 2432
W_IN_PACKED = O_GL + 3 * D

VMEM_BIG = 56 * 1024 * 1024


def _rms(xf, g):
    return xf * lax.rsqrt(jnp.mean(xf * xf, axis=-1, keepdims=True) + EPS) * g


def _dot(a, b):
    return jnp.dot(a, b, preferred_element_type=F32)


def _dot_nt(a, b):
    return lax.dot_general(a, b, (((1,), (1,)), ((), ())), preferred_element_type=F32)


def _const_spec(shape):
    nd = len(shape)
    return pl.BlockSpec(shape, lambda *_: (0,) * nd, pipeline_mode=pl.Buffered(1))


def _mod_index(tiles_per_batch):
    return lambda i: (jnp.minimum(i // tiles_per_batch, B), 0, 0)


def _stream_specs(split):
    nxt = NX // TM
    off = nxt if split else 0
    return [pl.BlockSpec((TM, D), lambda i: (jnp.minimum(i, nxt - 1), 0)),
            pl.BlockSpec((TM, D), lambda i: (jnp.maximum(i, nxt) - off, 0))]


def _stream_tile(ha_ref, hb_ref):
    return jnp.where(pl.program_id(0) < NX // TM, ha_ref[...], hb_ref[...])


def _mod_kernel(c_ref, w_ref, b_ref, o_ref):
    c = c_ref[...]
    sc = c * jax.nn.sigmoid(c)
    o_ref[0] = jnp.dot(sc, w_ref[0], preferred_element_type=F32, precision=lax.Precision.HIGHEST) + b_ref[0]


def _modulation(c_all, w_mod, b_mod):
    tn = 1536
    return pl.pallas_call(
        _mod_kernel,
        grid=(DEPTH, N_MOD * D // tn),
        in_specs=[
            pl.BlockSpec((16, D), lambda l, j: (0, 0)),
            pl.BlockSpec((1, D, tn), lambda l, j: (l, 0, j)),
            pl.BlockSpec((1, 1, tn), lambda l, j: (l, 0, j)),
        ],
        out_specs=pl.BlockSpec((1, 16, tn), lambda l, j: (l, 0, j)),
        out_shape=jax.ShapeDtypeStruct((DEPTH, 16, N_MOD * D), F32),
        compiler_params=pltpu.CompilerParams(dimension_semantics=("parallel", "parallel"), vmem_limit_bytes=VMEM_BIG),
        name="modulation",
    )(c_all, w_mod, b_mod.reshape(DEPTH, 1, N_MOD * D))


def _inproj_kernel(ha_ref, hb_ref, mod_ref, gpre_ref, win_ref, gq_ref, wqa_ref, wqb_ref, gkv_ref, wuk_ref, wuv_ref,
                   cos_ref, sin_ref, gv_ref, bv_ref, wsp_ref, bsp_ref,
                   q_ref, k_ref, v_ref, naq_ref, nak_ref, nav_ref, ycm_ref, sg_ref):
    n = _rms(_stream_tile(ha_ref, hb_ref), gpre_ref[...]) * (1.0 + mod_ref[0, 1:2, :]) + mod_ref[0, 0:1, :]
    nb = n.astype(BF16)

    def proj(a, b):
        return _dot(nb, win_ref[:, a:b])

    cos = cos_ref[...]
    sin = sin_ref[...]
    ones_hi = jnp.where(lax.broadcasted_iota(jnp.int32, (TM, LANE), 1) >= VD, 1.0, 0.0)

    cqn = _rms(proj(O_CQ, O_CKV), gq_ref[...]).astype(BF16)
    qa = _dot(cqn, wqa_ref[...])
    qb = _dot(cqn, wqb_ref[...])
    for hh in range(H_MLA):
        sl = slice(hh * LANE, (hh + 1) * LANE)
        q_ref[hh] = ((qa[:, sl] * cos + qb[:, sl] * sin) * (QK ** -0.5 * LOG2E)).astype(BF16)
    ckvn = _rms(proj(O_CKV, O_KRA), gkv_ref[...]).astype(BF16)
    krp = proj(O_KRA, O_KRB) * cos + proj(O_KRB, O_NAQ) * sin
    kk = _dot(ckvn, wuk_ref[...])
    for hh in range(H_MLA):
        k_ref[hh] = (kk[:, hh * LANE:(hh + 1) * LANE] + krp).astype(BF16)
    vv = _dot(ckvn, wuv_ref[...])
    for hh in range(H_MLA):
        v_ref[hh] = (vv[:, hh * LANE:(hh + 1) * LANE] + ones_hi).astype(BF16)

    naq = proj(O_NAQ, O_NAK) * (HD_NA ** -0.5 * LOG2E)
    nak = proj(O_NAK, O_NAV)
    for p in range(H_NA // 2):
        sl = slice(p * LANE, (p + 1) * LANE)
        naq_ref[p] = naq[:, sl].astype(BF16)
        nak_ref[p] = nak[:, sl].astype(BF16)
    nav = proj(O_NAV, O_U)
    for hh in range(H_NA):
        nav_ref[hh] = (nav[:, hh * LANE:(hh + 1) * LANE] + ones_hi).astype(BF16)

    u = proj(O_U, O_V)
    v = proj(O_V, O_GL)
    mu = jnp.mean(v, axis=-1, keepdims=True)
    vc = v - mu
    vn = (vc * lax.rsqrt(jnp.mean(vc * vc, axis=-1, keepdims=True) + EPS) * gv_ref[...] + bv_ref[...]).astype(BF16)
    grp = lax.broadcasted_iota(jnp.int32, (CM_CHUNK, CM_W), 1) // (CM_W // CM_G)
    for ch in range(TM // CM_CHUNK):
        rs = slice(ch * CM_CHUNK, (ch + 1) * CM_CHUNK)
        vch = vn[rs, :]
        sp = _dot(wsp_ref[CM_G - 1], vch)
        for g in range(CM_G - 2, -1, -1):
            sp = jnp.where(grp == g, _dot(wsp_ref[g], vch), sp)
        ycm_ref[rs, :] = (u[rs, :] * (sp + bsp_ref[...])).astype(BF16)

    for j in range(3):
        sg_ref[:, j * D:(j + 1) * D] = jax.nn.sigmoid(proj(O_GL + j * D, O_GL + (j + 1) * D)).astype(BF16)


def _inproj(h_parts, mod_l, gpre, wl, rope_cos, rope_sin):
    tiles_x = S // TM
    n_tiles = N // TM
    pos_idx = lambda i: (jnp.where(i < NX // TM, i % tiles_x, tiles_x), 0)
    row = lambda i: (i, 0)
    hrow = lambda i: (0, i, 0)
    in_specs = _stream_specs(h_parts[0] is not h_parts[1]) + [
        pl.BlockSpec((1, N_MOD, D), _mod_index(tiles_x)),
        _const_spec((1, D)),
        _const_spec((D, W_IN_PACKED)),
        _const_spec((1, Q_LORA)),
        _const_spec((Q_LORA, H_MLA * LANE)),
        _const_spec((Q_LORA, H_MLA * LANE)),
        _const_spec((1, KV_LORA)),
        _const_spec((KV_LORA, H_MLA * LANE)),
        _const_spec((KV_LORA, H_MLA * LANE)),
        pl.BlockSpec((TM, LANE), pos_idx),
        pl.BlockSpec((TM, LANE), pos_idx),
        _const_spec((1, CM_W)),
        _const_spec((1, CM_W)),
        _const_spec((CM_G, CM_CHUNK, CM_CHUNK)),
        _const_spec((CM_CHUNK, CM_W)),
    ]
    heads = lambda n: (jax.ShapeDtypeStruct((n, N, LANE), BF16), pl.BlockSpec((n, TM, LANE), hrow))
    outs = [
        heads(H_MLA), heads(H_MLA), heads(H_MLA), heads(H_NA // 2), heads(H_NA // 2), heads(H_NA),
        (jax.ShapeDtypeStruct((N, CM_W), BF16), pl.BlockSpec((TM, CM_W), row)),
        (jax.ShapeDtypeStruct((N, 3 * D), BF16), pl.BlockSpec((TM, 3 * D), row)),
    ]
    return pl.pallas_call(
        _inproj_kernel,
        grid=(n_tiles,),
        in_specs=in_specs,
        out_specs=[o[1] for o in outs],
        out_shape=[o[0] for o in outs],
        compiler_params=pltpu.CompilerParams(dimension_semantics=("parallel",), vmem_limit_bytes=VMEM_BIG),
        name="inproj",
    )(*h_parts, mod_l, gpre, wl["w_in"], wl["g_q"], wl["w_qa"], wl["w_qb"], wl["g_kv"], wl["w_uk"], wl["w_uv"],
      rope_cos, rope_sin, wl["g_v"], wl["b_v"], wl["w_sp"], wl["b_sp"])


def _pair_out(acc0, acc1, lane):
    return jnp.where(lane < VD, acc0 / pltpu.roll(acc0, VD, 1), pltpu.roll(acc1, VD, 1) / acc1)


def _mla_kernel(q_ref, kx_ref, kc_ref, vx_ref, vc_ref, o_ref, s_scr, *, n_qx, with_ctx):
    lane = lax.broadcasted_iota(jnp.int32, (TQ, LANE), 1)

    def step(with_x):
        n_k = C + (S if with_x else 0)
        for hh in range(MLA_HPS):
            q = q_ref[hh]
            s_scr[hh, :, 0:C] = _dot_nt(q, kc_ref[hh])
            if with_x:
                s_scr[hh, :, C:] = _dot_nt(q, kx_ref[hh])
        accs = []
        for hh in range(MLA_HPS):
            m = jnp.max(s_scr[hh, :, 0:n_k], axis=-1, keepdims=True)
            acc = _dot(jnp.exp2(s_scr[hh, :, 0:C] - m).astype(BF16), vc_ref[hh])
            if with_x:
                acc = acc + _dot(jnp.exp2(s_scr[hh, :, C:] - m).astype(BF16), vx_ref[hh])
            accs.append(acc)
        for p in range(MLA_HPS // 2):
            o_ref[:, p * LANE:(p + 1) * LANE] = _pair_out(accs[2 * p], accs[2 * p + 1], lane).astype(o_ref.dtype)

    if not with_ctx:
        step(True)
        return
    is_x = pl.program_id(2) < n_qx
    pl.when(is_x)(lambda: step(True))
    pl.when(jnp.logical_not(is_x))(lambda: step(False))


def _mla_attention(q, k, v, with_ctx):
    n_qx = S // TQ
    n_q = n_qx + (1 if with_ctx else 0)
    xq = NX // TQ
    qrow = lambda b, q_: jnp.where(q_ < n_qx, b * n_qx + q_, xq + b)
    hps = MLA_HPS
    xkeys = pl.BlockSpec((hps, S, LANE), lambda b, p, q_: (p, b, 0))
    ckeys = pl.BlockSpec((hps, C, LANE), lambda b, p, q_: (p, NX // C + b, 0))
    return pl.pallas_call(
        functools.partial(_mla_kernel, n_qx=n_qx, with_ctx=with_ctx),
        grid=(B, H_MLA // hps, n_q),
        in_specs=[pl.BlockSpec((hps, TQ, LANE), lambda b, p, q_: (p, qrow(b, q_), 0)), xkeys, ckeys, xkeys, ckeys],
        out_specs=pl.BlockSpec((TQ, hps * VD), lambda b, p, q_: (qrow(b, q_), p)),
        out_shape=jax.ShapeDtypeStruct((N if with_ctx else NX, H_MLA * VD), BF16),
        scratch_shapes=[pltpu.VMEM((hps, TQ, C + S), F32)],
        compiler_params=pltpu.CompilerParams(dimension_semantics=("parallel", "parallel", "arbitrary"),
                                             vmem_limit_bytes=VMEM_BIG),
        name="mla_attention",
    )(q, k, k, v, v)


def _na_kernel(q_ref, kx_ref, kc_ref, vx_ref, vc_ref, bias_ref, o_ref, s_scr, *, n_g, with_ctx):
    lane = lax.broadcasted_iota(jnp.int32, (TQ, LANE), 1)

    def step(windowed):
        n_w = NA_KROWS * GRID_W
        n_k = C + (n_w if windowed else 0)
        if windowed:
            base = jnp.clip(pl.program_id(1) * NA_QROWS - WIN_R // 2, 0, ROWS - NA_KROWS)
            ks = pl.ds(pl.multiple_of(base * GRID_W, GRID_W), n_w)
        for hh in range(H_NA):
            p, half = divmod(hh, 2)
            qp = q_ref[p]
            q = jnp.where((lane < HD_NA) == (half == 0), qp, jnp.zeros_like(qp))
            s_scr[hh, :, 0:C] = _dot_nt(q, kc_ref[p])
            if windowed:
                s_scr[hh, :, C:] = _dot_nt(q, kx_ref[p, ks, :]) + bias_ref[0, hh]
        accs = []
        for hh in range(H_NA):
            m = jnp.max(s_scr[hh, :, 0:n_k], axis=-1, keepdims=True)
            acc = _dot(jnp.exp2(s_scr[hh, :, 0:C] - m).astype(BF16), vc_ref[hh])
            if windowed:
                acc = acc + _dot(jnp.exp2(s_scr[hh, :, C:] - m).astype(BF16), vx_ref[hh, ks, :])
            accs.append(acc)
        for p in range(H_NA // 2):
            o_ref[:, p * LANE:(p + 1) * LANE] = _pair_out(accs[2 * p], accs[2 * p + 1], lane).astype(o_ref.dtype)

    if not with_ctx:
        step(True)
        return
    is_x = pl.program_id(1) < n_g
    pl.when(is_x)(lambda: step(True))
    pl.when(jnp.logical_not(is_x))(lambda: step(False))


def _na_attention(q, k, v, bias, with_ctx):
    n_g = ROWS // NA_QROWS
    n_steps = n_g + (1 if with_ctx else 0)
    xq = NX // TQ
    qrow = lambda b, g: jnp.where(g < n_g, b * n_g + g, xq + b)
    kind = lambda b, g: (jnp.where(g == 0, 0, jnp.where(g >= n_g - 1, 2, 1)), 0, 0, 0)
    hp = H_NA // 2
    return pl.pallas_call(
        functools.partial(_na_kernel, n_g=n_g, with_ctx=with_ctx),
        grid=(B, n_steps),
        in_specs=[
            pl.BlockSpec((hp, TQ, LANE), lambda b, g: (0, qrow(b, g), 0)),
            pl.BlockSpec((hp, S, LANE), lambda b, g: (0, b, 0)),
            pl.BlockSpec((hp, C, LANE), lambda b, g: (0, NX // C + b, 0)),
            pl.BlockSpec((H_NA, S, LANE), lambda b, g: (0, b, 0)),
            pl.BlockSpec((H_NA, C, LANE), lambda b, g: (0, NX // C + b, 0)),
            pl.BlockSpec((1, H_NA, TQ, NA_KROWS * GRID_W), kind),
        ],
        out_specs=pl.BlockSpec((TQ, NA_W), lambda b, g: (qrow(b, g), 0)),
        out_shape=jax.ShapeDtypeStruct((N if with_ctx else NX, NA_W), BF16),
        scratch_shapes=[pltpu.VMEM((H_NA, TQ, C + NA_KROWS * GRID_W), F32)],
        compiler_params=pltpu.CompilerParams(dimension_semantics=("parallel", "arbitrary"),
                                             vmem_limit_bytes=VMEM_BIG),
        name="na_attention",
    )(q, k, k, v, v, bias)


def _na_bias_table(rpb_l):
    qc = np.arange(GRID_W)
    kc = np.arange(GRID_W)
    ws = np.clip(qc - WIN_C // 2, 0, GRID_W - WIN_C)
    col_ok = (kc[None, :] >= ws[:, None]) & (kc[None, :] < ws[:, None] + WIN_C)
    dc = np.clip(kc[None, :] - qc[:, None] + WIN_C - 1, 0, 2 * WIN_C - 2)
    sel_c = (dc[None] == np.arange(2 * WIN_C - 1)[:, None, None]).astype(np.float32)
    n_g = ROWS // NA_QROWS
    sel_r, oks = [], []
    for g in (0, 1, n_g - 1):
        base = int(np.clip(g * NA_QROWS - WIN_R // 2, 0, ROWS - NA_KROWS))
        r = g * NA_QROWS + np.arange(NA_QROWS)
        rs = np.clip(r - WIN_R // 2, 0, ROWS - WIN_R)
        kr = base + np.arange(NA_KROWS)
        row_ok = (kr[None, :] >= rs[:, None]) & (kr[None, :] < rs[:, None] + WIN_R)
        ri = kr[None, :] - r[:, None] + WIN_R - 1
        sel_r.append(((ri[:, :, None] == np.arange(2 * WIN_R - 1)) & row_ok[:, :, None]).astype(np.float32))
        oks.append(row_ok[:, None, :, None] & col_ok[None, :, None, :])
    hi = lax.Precision.HIGHEST
    by_col = jnp.einsum("hrd,dqk->hrqk", rpb_l * LOG2E, jnp.asarray(sel_c), precision=hi)
    bias = jnp.einsum("glar,hrqk->ghlqak", jnp.asarray(np.stack(sel_r)), by_col, precision=hi)
    bias = jnp.where(jnp.asarray(np.stack(oks))[:, None], bias, NEG)
    return bias.reshape(3, H_NA, TQ, NA_KROWS * GRID_W).astype(F32)


def _merge_core(ym_ref, yn_ref, yc_ref, sg_ref, ha_ref, hb_ref, mod_ref, gpost_ref, gffn_ref, wbr_ref, wout_ref):
    o1, o2 = H_MLA * VD, H_MLA * VD + NA_W
    z = (sg_ref[:, 0:D].astype(F32) * _dot(ym_ref[...], wbr_ref[0:o1, :])
         + sg_ref[:, D:2 * D].astype(F32) * _dot(yn_ref[...], wbr_ref[o1:o2, :])
         + sg_ref[:, 2 * D:3 * D].astype(F32) * _dot(yc_ref[...], wbr_ref[o2:, :]))
    o = _dot(z.astype(BF16), wout_ref[...])
    h1 = _stream_tile(ha_ref, hb_ref) + mod_ref[0, 2:3, :] * _rms(o, gpost_ref[...])
    t = _rms(h1, gffn_ref[...]) * (1.0 + mod_ref[0, 4:5, :]) + mod_ref[0, 3:4, :]
    return h1, t


def _merge_ffn_kernel(ym_ref, yn_ref, yc_ref, sg_ref, ha_ref, hb_ref, mod_ref, gpost_ref, gffn_ref, wbr_ref,
                      wout_ref, gpf_ref, w1_ref, w3_ref, w2_ref, o_ref):
    h1, t = _merge_core(ym_ref, yn_ref, yc_ref, sg_ref, ha_ref, hb_ref, mod_ref, gpost_ref, gffn_ref, wbr_ref,
                        wout_ref)
    tb = t.astype(BF16)
    half = D_FF // 2
    f = None
    for j in range(2):
        sl = slice(j * half, (j + 1) * half)
        a = _dot(tb, w1_ref[:, sl])
        b = _dot(tb, w3_ref[:, sl])
        part = _dot((a * jax.nn.sigmoid(a) * b).astype(BF16), w2_ref[sl, :])
        f = part if f is None else f + part
    o_ref[...] = h1 + mod_ref[0, 5:6, :] * _rms(f, gpf_ref[...])


def _merge_route_kernel(ym_ref, yn_ref, yc_ref, sg_ref, ha_ref, hb_ref, mod_ref, gpost_ref, gffn_ref, wbr_ref,
                        wout_ref, wr_ref, h1_ref, t_ref, e_ref, w_ref, cnt_ref):
    h1, t = _merge_core(ym_ref, yn_ref, yc_ref, sg_ref, ha_ref, hb_ref, mod_ref, gpost_ref, gffn_ref, wbr_ref,
                        wout_ref)
    h1_ref[...] = h1
    t_ref[...] = t
    t_hi = t.astype(BF16)
    t_lo = (t - t_hi.astype(F32)).astype(BF16)
    hi = _dot(t_hi, wr_ref[...])
    logits = hi[:, :LANE] + hi[:, LANE:] + _dot(t_lo, wr_ref[:, :LANE])
    lane = lax.broadcasted_iota(jnp.int32, logits.shape, 1)
    logits = jnp.where(lane < N_EXP, logits, -jnp.inf)
    l1 = jnp.max(logits, axis=-1, keepdims=True)
    e1 = jnp.min(jnp.where(logits == l1, lane, LANE), axis=-1, keepdims=True)
    rest = jnp.where(lane == e1, -jnp.inf, logits)
    l2 = jnp.max(rest, axis=-1, keepdims=True)
    e2 = jnp.min(jnp.where(rest == l2, lane, LANE), axis=-1, keepdims=True)
    ex = jnp.exp(l2 - l1)
    den = 1.0 + ex
    w_ref[...] = jnp.where(lane == 0, 1.0 / den, jnp.where(lane == 1, ex / den, 0.0))

    @pl.when(pl.program_id(0) == 0)
    def _():
        cnt_ref[...] = jnp.zeros(cnt_ref.shape, F32)

    oh1 = lane == e1
    oh2 = lane == e2
    both = jnp.where(oh1 | oh2, 1.0, 0.0)
    r_i = lax.broadcasted_iota(jnp.int32, (TM, TM), 0)
    c_i = lax.broadcasted_iota(jnp.int32, (TM, TM), 1)
    tri = jnp.where(c_i < r_i, 1.0, 0.0).astype(BF16)
    run = cnt_ref[0:1, :]
    before = _dot(tri, both.astype(BF16)) + run
    rank1 = jnp.sum(jnp.where(oh1, before, 0.0), axis=-1, keepdims=True).astype(jnp.int32)
    rank2 = jnp.sum(jnp.where(oh2, before, 0.0), axis=-1, keepdims=True).astype(jnp.int32)
    e_ref[...] = jnp.where(lane == 0, e1, jnp.where(lane == 1, e2, jnp.where(lane == 2, rank1,
                                                                               jnp.where(lane == 3, rank2, 0))))
    cnt_ref[...] = jnp.broadcast_to(run + jnp.sum(both, axis=0, keepdims=True), cnt_ref.shape)


def _merge_specs(h_parts):
    row = lambda i: (i, 0)
    return [
        pl.BlockSpec((TM, H_MLA * VD), row),
        pl.BlockSpec((TM, NA_W), row),
        pl.BlockSpec((TM, CM_W), row),
        pl.BlockSpec((TM, 3 * D), row),
    ] + _stream_specs(h_parts[0] is not h_parts[1]) + [
        pl.BlockSpec((1, N_MOD, D), _mod_index(S // TM)),
        _const_spec((1, D)),
        _const_spec((1, D)),
        _const_spec((D, D)),
        _const_spec((D, D)),
    ]


def _merge_ffn(ym, yn, yc, sg, h_parts, mod_l, gpost, gffn, wl, gpf, w1, w3, w2):
    row = lambda i: (i, 0)
    return pl.pallas_call(
        _merge_ffn_kernel,
        grid=(N // TM,),
        in_specs=_merge_specs(h_parts) + [_const_spec((1, D)), _const_spec((D, D_FF)), _const_spec((D, D_FF)),
                                          _const_spec((D_FF, D))],
        out_specs=pl.BlockSpec((TM, D), row),
        out_shape=jax.ShapeDtypeStruct((N, D), F32),
        compiler_params=pltpu.CompilerParams(dimension_semantics=("parallel",), vmem_limit_bytes=VMEM_BIG),
        name="merge_ffn",
    )(ym, yn, yc, sg, *h_parts, mod_l, gpost, gffn, wl["w_br"], wl["w_out"], gpf, w1, w3, w2)


def _merge_route(ym, yn, yc, sg, h_parts, mod_l, gpost, gffn, wl, w_router):
    row = lambda i: (i, 0)
    rows = lambda width, dtype: (jax.ShapeDtypeStruct((NX, width), dtype), pl.BlockSpec((TM, width), row))
    outs = [rows(D, F32), rows(D, F32), rows(LANE, jnp.int32), rows(LANE, F32),
            (jax.ShapeDtypeStruct((8, LANE), F32), pl.BlockSpec((8, LANE), lambda i: (0, 0)))]
    return pl.pallas_call(
        _merge_route_kernel,
        grid=(NX // TM,),
        in_specs=_merge_specs(h_parts) + [_const_spec((D, 2 * LANE))],
        out_specs=[o[1] for o in outs],
        out_shape=[o[0] for o in outs],
        compiler_params=pltpu.CompilerParams(dimension_semantics=("arbitrary",), vmem_limit_bytes=VMEM_BIG),
        name="merge_route",
    )(ym, yn, yc, sg, *h_parts, mod_l, gpost, gffn, wl["w_br"], wl["w_out"], w_router)


def _moe_kernel(be_ref, nu_ref, x_ref, w1_hbm, w3_hbm, w2_hbm, y_in_hbm, y_ref, wb1, wb3, wb2, st_in, st_out, sem,
                *, blk0):
    del y_in_hbm
    j = pl.program_id(0) + blk0
    n_f = D_FFE // FT
    e = be_ref[j]
    fresh = jnp.logical_or(pl.program_id(0) == 0, e != be_ref[jnp.maximum(j - 1, 0)])
    used = j < nu_ref[0]

    def ff_chunk(x, f):
        sl = slice(f * FT, (f + 1) * FT)
        a = _dot(x, wb1[:, sl])
        b = _dot(x, wb3[:, sl])
        part = _dot((a * jax.nn.sigmoid(a) * b).astype(BF16), wb2[sl, :])
        if f == 0:
            y_ref[...] = part
        else:
            y_ref[...] += part

    @pl.when(jnp.logical_and(fresh, used))
    def _():
        def chunk_copy(k):
            f, mat = divmod(k, 3)
            slot = k % 2
            cols = pl.ds(f * FT, FT)
            if mat == 2:
                return pltpu.make_async_copy(w2_hbm.at[e, cols, :], st_out.at[slot], sem.at[slot])
            return pltpu.make_async_copy((w1_hbm, w3_hbm)[mat].at[e, :, cols], st_in.at[slot], sem.at[slot])

        x = x_ref[...].astype(BF16)
        chunk_copy(0).start()
        for k in range(3 * n_f):
            if k + 1 < 3 * n_f:
                chunk_copy(k + 1).start()
            chunk_copy(k).wait()
            f, mat = divmod(k, 3)
            slot = k % 2
            sl = slice(f * FT, (f + 1) * FT)
            if mat == 0:
                wb1[:, sl] = st_in[slot].astype(BF16)
            elif mat == 1:
                wb3[:, sl] = st_in[slot].astype(BF16)
            else:
                wb2[sl, :] = st_out[slot].astype(BF16)
                ff_chunk(x, f)

    @pl.when(jnp.logical_and(jnp.logical_not(fresh), used))
    def _():
        x = x_ref[...].astype(BF16)
        for f in range(n_f):
            ff_chunk(x, f)

    @pl.when(jnp.logical_not(used))
    def _():
        y_ref[...] = jnp.zeros(y_ref.shape, y_ref.dtype)


def _moe_experts(x_part, y_prev, part, blk_expert, n_used, w1, w3, w2):
    pb = N_BLK // MOE_PARTS
    blk0 = part * pb
    hbm = pl.BlockSpec(memory_space=pl.ANY)
    args = [blk_expert, n_used, x_part, w1, w3, w2, y_prev]
    aliases = {len(args) - 1: 0}
    grid_spec = pltpu.PrefetchScalarGridSpec(
        num_scalar_prefetch=2,
        grid=(pb,),
        in_specs=[pl.BlockSpec((MOE_BLK, D), lambda j, be, nu: (j, 0)), hbm, hbm, hbm, hbm],
        out_specs=pl.BlockSpec((MOE_BLK, D), lambda j, be, nu: (j + blk0, 0)),
        scratch_shapes=[
            pltpu.VMEM((D, D_FFE), BF16), pltpu.VMEM((D, D_FFE), BF16), pltpu.VMEM((D_FFE, D), BF16),
            pltpu.VMEM((2, D, FT), F32), pltpu.VMEM((2, FT, D), F32), pltpu.SemaphoreType.DMA((2,)),
        ],
    )
    return pl.pallas_call(
        functools.partial(_moe_kernel, blk0=blk0),
        grid_spec=grid_spec,
        out_shape=jax.ShapeDtypeStruct((N_PAD, D), F32),
        input_output_aliases=aliases,
        compiler_params=pltpu.CompilerParams(dimension_semantics=("arbitrary",), vmem_limit_bytes=VMEM_BIG),
        name="moe_experts",
    )(*args)


def _combine_kernel(ya_ref, yb_ref, w_ref, h1_ref, mod_ref, g_ref, o_ref):
    f = w_ref[:, 0:1] * ya_ref[...] + w_ref[:, 1:2] * yb_ref[...]
    o_ref[...] = h1_ref[...] + mod_ref[0, 5:6, :] * _rms(f, g_ref[...])


def _moe_combine(ya, yb, w, h1, mod_l, g):
    row = lambda i: (i, 0)
    return pl.pallas_call(
        _combine_kernel,
        grid=(NX // TM,),
        in_specs=[
            pl.BlockSpec((TM, D), row),
            pl.BlockSpec((TM, D), row),
            pl.BlockSpec((TM, LANE), row),
            pl.BlockSpec((TM, D), row),
            pl.BlockSpec((1, N_MOD, D), _mod_index(S // TM)),
            _const_spec((1, D)),
        ],
        out_specs=pl.BlockSpec((TM, D), row),
        out_shape=jax.ShapeDtypeStruct((NX, D), F32),
        compiler_params=pltpu.CompilerParams(dimension_semantics=("parallel",), vmem_limit_bytes=VMEM_BIG),
        name="moe_combine",
    )(ya, yb, w, h1, mod_l, g)


def _moe_layout(e, rank, counts):
    padded = (counts + MOE_BLK - 1) // MOE_BLK * MOE_BLK
    pad_end = jnp.cumsum(padded)
    pad_start = pad_end - padded
    onehot = e[:, :, None] == jnp.arange(N_EXP, dtype=jnp.int32)
    dest = jnp.sum(jnp.where(onehot, pad_start, 0), axis=-1) + rank
    blk_start = jnp.arange(N_BLK, dtype=jnp.int32) * MOE_BLK
    blk_expert = jnp.minimum(jnp.sum((blk_start[:, None] >= pad_end[None, :]).astype(jnp.int32), axis=1), N_EXP - 1)
    n_used = (pad_end[-1] // MOE_BLK).astype(jnp.int32).reshape(1)
    _, tok_sorted = lax.sort_key_val(dest.reshape(N_ASG), jnp.arange(N_ASG, dtype=jnp.int32) // 2)
    squeeze = pad_start - (jnp.cumsum(counts) - counts)
    slot = jnp.arange(N_PAD, dtype=jnp.int32)
    compact = slot - jnp.repeat(squeeze[blk_expert], MOE_BLK)
    slot_tok = tok_sorted.at[jnp.clip(compact, 0, N_ASG - 1)].get(mode="promise_in_bounds")
    return dest.astype(jnp.int32), slot_tok, blk_expert.astype(jnp.int32), n_used


def _rope_rot(w):
    half, quarter = ROPE // 2, ROPE // 4
    return jnp.concatenate([-w[..., quarter:half], w[..., :quarter], -w[..., half + quarter:], w[..., half:half + quarter]],
                           axis=-1)


def _rope_tables():
    half = ROPE // 2
    inv = ROPE_BASE ** (-(jnp.arange(0, half, 2, dtype=F32) / half))
    t = jnp.arange(S)
    ang_r = (t // GRID_W).astype(F32)[:, None] * inv
    ang_c = (t % GRID_W).astype(F32)[:, None] * inv
    ang = jnp.concatenate([ang_r, ang_r, ang_c, ang_c], axis=-1)
    cos32, sin32 = jnp.cos(ang), jnp.sin(ang)
    ones = lambda n: jnp.ones((S, n), F32)
    zeros = lambda n: jnp.zeros((S, n), F32)
    cos = jnp.concatenate([ones(NOPE), cos32, ones(LANE - QK)], axis=1)
    sin = jnp.concatenate([zeros(NOPE), sin32, zeros(LANE - QK)], axis=1)
    return (jnp.concatenate([cos, jnp.ones((TM, LANE), F32)], axis=0),
            jnp.concatenate([sin, jnp.zeros((TM, LANE), F32)], axis=0))


def _pad_heads(w, n_heads, width):
    w = w.reshape(w.shape[0], n_heads, width)
    return jnp.concatenate([w, jnp.zeros((w.shape[0], n_heads, LANE - width), w.dtype)], axis=2).reshape(w.shape[0], -1)


def _pack_layer(l, w_in, g_q, w_uq, g_kv, w_ukv, g_v, b_v, w_sp, b_sp, w_br, w_out):
    wi = w_in[l]
    offs = np.cumsum((0, Q_LORA, KV_LORA, ROPE, NA_W, NA_W, NA_W, CM_W, CM_W))
    cq, ckv, kr, naq, nak, nav, u, v = (wi[:, offs[i]:offs[i + 1]] for i in range(8))
    gl = wi[:, offs[8]:]
    zl = jnp.zeros((D, NOPE), F32)
    zr = jnp.zeros((D, LANE - NOPE - ROPE), F32)
    w_in_p = jnp.concatenate([cq, ckv, zl, kr, zr, zl, _rope_rot(kr), zr, naq, nak, _pad_heads(nav, H_NA, HD_NA),
                              u, v, gl], axis=1).astype(BF16)
    uq = w_uq[l].reshape(Q_LORA, H_MLA, QK)
    zq = jnp.zeros((Q_LORA, H_MLA, LANE - QK), F32)
    w_qa = jnp.concatenate([uq, zq], axis=2).reshape(Q_LORA, H_MLA * LANE).astype(BF16)
    w_qb = jnp.concatenate([jnp.zeros((Q_LORA, H_MLA, NOPE), F32), _rope_rot(uq[:, :, NOPE:]), zq], axis=2)
    w_qb = w_qb.reshape(Q_LORA, H_MLA * LANE).astype(BF16)
    ukv = w_ukv[l].reshape(KV_LORA, H_MLA, NOPE + VD)
    w_uk = _pad_heads(ukv[:, :, :NOPE].reshape(KV_LORA, -1), H_MLA, NOPE).astype(BF16)
    w_uv = _pad_heads(ukv[:, :, NOPE:].reshape(KV_LORA, -1), H_MLA, VD).astype(BF16)
    b_sp_t = jnp.repeat(b_sp[l].T, CM_W // CM_G, axis=1)
    return dict(
        w_in=w_in_p, g_q=g_q[l].reshape(1, -1), w_qa=w_qa, w_qb=w_qb, g_kv=g_kv[l].reshape(1, -1), w_uk=w_uk,
        w_uv=w_uv, g_v=g_v[l].reshape(1, -1), b_v=b_v[l].reshape(1, -1), w_sp=w_sp[l].astype(BF16), b_sp=b_sp_t,
        w_br=w_br[l].astype(BF16), w_out=w_out[l].astype(BF16))


def kernel(x, c, ctx, c_ctx, w_mod, b_mod, g_pre_mix, g_post_mix, g_pre_ffn, g_post_ffn, w_in, g_q, w_uq, g_kv, w_ukv, rpb, g_v, b_v, w_sp, b_sp, w_br, w_out, w_ffn1, w_ffn3, w_ffn2, w_router, w_moe1, w_moe3, w_moe2):
    assert DEPTH == 2, "layer 0 is the dense layer with context outputs, layer 1 the last (expert) layer"
    h_parts = (x.reshape(NX, D), ctx.reshape(NC, D))
    c_all = jnp.concatenate([c, c_ctx[None, :], jnp.zeros((16 - B - 1, D), F32)], axis=0)
    mod = _modulation(c_all, w_mod, b_mod).reshape(DEPTH, 16, N_MOD, D)
    rope_cos, rope_sin = _rope_tables()
    row1 = lambda a: a.reshape(1, -1)

    for l in range(DEPTH):
        last = l == DEPTH - 1
        wl = _pack_layer(l, w_in, g_q, w_uq, g_kv, w_ukv, g_v, b_v, w_sp, b_sp, w_br, w_out)
        mod_l = mod[l]
        q, k, v, naq, nak, nav, ycm, sg = _inproj(h_parts, mod_l, row1(g_pre_mix[l]), wl, rope_cos, rope_sin)
        ym = _mla_attention(q, k, v, with_ctx=not last)
        yn = _na_attention(naq, nak, nav, _na_bias_table(rpb[l]), with_ctx=not last)
        i = l // 2
        if not last:
            h = _merge_ffn(ym, yn, ycm, sg, h_parts, mod_l, row1(g_post_mix[l]), row1(g_pre_ffn[l]), wl,
                           row1(g_post_ffn[l]), w_ffn1[i].astype(BF16), w_ffn3[i].astype(BF16),
                           w_ffn2[i].astype(BF16))
            h_parts = (h, h)
        else:
            wr_hi = w_router[i].astype(BF16)
            wr_lo = (w_router[i] - wr_hi.astype(F32)).astype(BF16)
            zpad = jnp.zeros((D, LANE - N_EXP), BF16)
            wr = jnp.concatenate([wr_hi, zpad, wr_lo, zpad], axis=1)
            h1, t, e, w, cnt = _merge_route(ym, yn, ycm, sg, h_parts, mod_l, row1(g_post_mix[l]),
                                            row1(g_pre_ffn[l]), wl, wr)
            dest, slot_tok, blk_expert, n_used = _moe_layout(e[:, 0:2], e[:, 2:4], cnt[0, :N_EXP].astype(jnp.int32))
            inb = dict(mode="promise_in_bounds")
            y = jnp.zeros((N_PAD, D), F32)
            part_rows = N_PAD // MOE_PARTS
            for part in range(MOE_PARTS):
                x_part = t.at[slot_tok[part * part_rows:(part + 1) * part_rows]].get(**inb)
                y = _moe_experts(x_part, y, part, blk_expert, n_used, w_moe1[i], w_moe3[i], w_moe2[i])
            h = _moe_combine(y.at[dest[:, 0]].get(**inb), y.at[dest[:, 1]].get(**inb), w, h1, mod_l,
                             row1(g_post_ffn[l]))
    return h[:NX].reshape(B, S, D)
```

```python
import functools

import numpy as np
import jax
import jax.numpy as jnp
from jax import lax
from jax.experimental import pallas as pl
from jax.experimental.pallas import tpu as pltpu

F32 = jnp.float32
BF16 = jnp.bfloat16

D = 1024
B = 8
S = 4096
DEPTH = 2
GRID_W = 64
ROWS = S // GRID_W
C = 256
EPS = 1e-6
NEG = -1e30
LOG2E = 1.4426950408889634
N_MOD = 6

H_MLA = 8
Q_LORA = 384
KV_LORA = 256
NOPE = 64
ROPE = 32
VD = 64
QK = NOPE + ROPE
ROPE_BASE = 10000.0

H_NA = 4
HD_NA = 64
NA_W = H_NA * HD_NA
WIN_R = 8
WIN_C = 16
NA_QROWS = 4
NA_KROWS = 12

CM_G = 4
CM_CHUNK = 128
CM_W = 256

D_FF = 2816
N_EXP = 8
D_FFE = 3584
MOE_BLK = 512

NX = B * S
NC = B * C
N = NX + NC
LANE = 128

TM = 512
TQ = 256
MLA_HPS = 4
FT = 512
N_ASG = NX * 2
N_BLK = N_ASG // MOE_BLK + N_EXP
N_PAD = N_BLK * MOE_BLK

O_CQ, O_CKV, O_KRA, O_KRB, O_NAQ, O_NAK, O_NAV, O_U, O_V, O_GL = 0, 384, 640, 768, 896, 1152, 1408, 1920, 2176, 2432
W_IN_PACKED = O_GL + 3 * D

VMEM_BIG = 56 * 1024 * 1024


def _rms(xf, g):
    return xf * lax.rsqrt(jnp.mean(xf * xf, axis=-1, keepdims=True) + EPS) * g


def _dot(a, b):
    return jnp.dot(a, b, preferred_element_type=F32)


def _dot_nt(a, b):
    return lax.dot_general(a, b, (((1,), (1,)), ((), ())), preferred_element_type=F32)


def _const_spec(shape):
    nd = len(shape)
    return pl.BlockSpec(shape, lambda *_: (0,) * nd, pipeline_mode=pl.Buffered(1))


def _mod_index(tiles_per_batch):
    return lambda i: (jnp.minimum(i // tiles_per_batch, B), 0, 0)


def _stream_specs(split):
    nxt = NX // TM
    off = nxt if split else 0
    return [pl.BlockSpec((TM, D), lambda i: (jnp.minimum(i, nxt - 1), 0)),
            pl.BlockSpec((TM, D), lambda i: (jnp.maximum(i, nxt) - off, 0))]


def _stream_tile(ha_ref, hb_ref):
    return jnp.where(pl.program_id(0) < NX // TM, ha_ref[...], hb_ref[...])


def _mod_kernel(c_ref, w_ref, b_ref, o_ref):
    c = c_ref[...]
    sc = c * jax.nn.sigmoid(c)
    o_ref[0] = jnp.dot(sc, w_ref[0], preferred_element_type=F32, precision=lax.Precision.HIGHEST) + b_ref[0]


def _modulation(c_all, w_mod, b_mod):
    tn = 1536
    return pl.pallas_call(
        _mod_kernel,
        grid=(DEPTH, N_MOD * D // tn),
        in_specs=[
            pl.BlockSpec((16, D), lambda l, j: (0, 0)),
            pl.BlockSpec((1, D, tn), lambda l, j: (l, 0, j)),
            pl.BlockSpec((1, 1, tn), lambda l, j: (l, 0, j)),
        ],
        out_specs=pl.BlockSpec((1, 16, tn), lambda l, j: (l, 0, j)),
        out_shape=jax.ShapeDtypeStruct((DEPTH, 16, N_MOD * D), F32),
        compiler_params=pltpu.CompilerParams(dimension_semantics=("parallel", "parallel"), vmem_limit_bytes=VMEM_BIG),
        name="modulation",
    )(c_all, w_mod, b_mod.reshape(DEPTH, 1, N_MOD * D))


def _inproj_kernel(ha_ref, hb_ref, mod_ref, gpre_ref, win_ref, gq_ref, wqa_ref, wqb_ref, gkv_ref, wuk_ref, wuv_ref,
                   cos_ref, sin_ref, gv_ref, bv_ref, wsp_ref, bsp_ref,
                   q_ref, k_ref, v_ref, naq_ref, nak_ref, nav_ref, ycm_ref, sg_ref):
    n = _rms(_stream_tile(ha_ref, hb_ref), gpre_ref[...]) * (1.0 + mod_ref[0, 1:2, :]) + mod_ref[0, 0:1, :]
    nb = n.astype(BF16)

    def proj(a, b):
        return _dot(nb, win_ref[:, a:b])

    cos = cos_ref[...]
    sin = sin_ref[...]
    ones_hi = jnp.where(lax.broadcasted_iota(jnp.int32, (TM, LANE), 1) >= VD, 1.0, 0.0)

    cqn = _rms(proj(O_CQ, O_CKV), gq_ref[...]).astype(BF16)
    qa = _dot(cqn, wqa_ref[...])
    qb = _dot(cqn, wqb_ref[...])
    for hh in range(H_MLA):
        sl = slice(hh * LANE, (hh + 1) * LANE)
        q_ref[hh] = ((qa[:, sl] * cos + qb[:, sl] * sin) * (QK ** -0.5 * LOG2E)).astype(BF16)
    ckvn = _rms(proj(O_CKV, O_KRA), gkv_ref[...]).astype(BF16)
    krp = proj(O_KRA, O_KRB) * cos + proj(O_KRB, O_NAQ) * sin
    kk = _dot(ckvn, wuk_ref[...])
    for hh in range(H_MLA):
        k_ref[hh] = (kk[:, hh * LANE:(hh + 1) * LANE] + krp).astype(BF16)
    vv = _dot(ckvn, wuv_ref[...])
    for hh in range(H_MLA):
        v_ref[hh] = (vv[:, hh * LANE:(hh + 1) * LANE] + ones_hi).astype(BF16)

    naq = proj(O_NAQ, O_NAK) * (HD_NA ** -0.5 * LOG2E)
    nak = proj(O_NAK, O_NAV)
    for p in range(H_NA // 2):
        sl = slice(p * LANE, (p + 1) * LANE)
        naq_ref[p] = naq[:, sl].astype(BF16)
        nak_ref[p] = nak[:, sl].astype(BF16)
    nav = proj(O_NAV, O_U)
    for hh in range(H_NA):
        nav_ref[hh] = (nav[:, hh * LANE:(hh + 1) * LANE] + ones_hi).astype(BF16)

    u = proj(O_U, O_V)
    v = proj(O_V, O_GL)
    mu = jnp.mean(v, axis=-1, keepdims=True)
    vc = v - mu
    vn = (vc * lax.rsqrt(jnp.mean(vc * vc, axis=-1, keepdims=True) + EPS) * gv_ref[...] + bv_ref[...]).astype(BF16)
    grp = lax.broadcasted_iota(jnp.int32, (CM_CHUNK, CM_W), 1) // (CM_W // CM_G)
    for ch in range(TM // CM_CHUNK):
        rs = slice(ch * CM_CHUNK, (ch + 1) * CM_CHUNK)
        vch = vn[rs, :]
        sp = _dot(wsp_ref[CM_G - 1], vch)
        for g in range(CM_G - 2, -1, -1):
            sp = jnp.where(grp == g, _dot(wsp_ref[g], vch), sp)
        ycm_ref[rs, :] = (u[rs, :] * (sp + bsp_ref[...])).astype(BF16)

    for j in range(3):
        sg_ref[:, j * D:(j + 1) * D] = jax.nn.sigmoid(proj(O_GL + j * D, O_GL + (j + 1) * D)).astype(BF16)


def _inproj(h_parts, mod_l, gpre, wl, rope_cos, rope_sin):
    tiles_x = S // TM
    n_tiles = N // TM
    pos_idx = lambda i: (jnp.where(i < NX // TM, i % tiles_x, tiles_x), 0)
    row = lambda i: (i, 0)
    hrow = lambda i: (0, i, 0)
    in_specs = _stream_specs(h_parts[0] is not h_parts[1]) + [
        pl.BlockSpec((1, N_MOD, D), _mod_index(tiles_x)),
        _const_spec((1, D)),
        _const_spec((D, W_IN_PACKED)),
        _const_spec((1, Q_LORA)),
        _const_spec((Q_LORA, H_MLA * LANE)),
        _const_spec((Q_LORA, H_MLA * LANE)),
        _const_spec((1, KV_LORA)),
        _const_spec((KV_LORA, H_MLA * LANE)),
        _const_spec((KV_LORA, H_MLA * LANE)),
        pl.BlockSpec((TM, LANE), pos_idx),
        pl.BlockSpec((TM, LANE), pos_idx),
        _const_spec((1, CM_W)),
        _const_spec((1, CM_W)),
        _const_spec((CM_G, CM_CHUNK, CM_CHUNK)),
        _const_spec((CM_CHUNK, CM_W)),
    ]
    heads = lambda n: (jax.ShapeDtypeStruct((n, N, LANE), BF16), pl.BlockSpec((n, TM, LANE), hrow))
    outs = [
        heads(H_MLA), heads(H_MLA), heads(H_MLA), heads(H_NA // 2), heads(H_NA // 2), heads(H_NA),
        (jax.ShapeDtypeStruct((N, CM_W), BF16), pl.BlockSpec((TM, CM_W), row)),
        (jax.ShapeDtypeStruct((N, 3 * D), BF16), pl.BlockSpec((TM, 3 * D), row)),
    ]
    return pl.pallas_call(
        _inproj_kernel,
        grid=(n_tiles,),
        in_specs=in_specs,
        out_specs=[o[1] for o in outs],
        out_shape=[o[0] for o in outs],
        compiler_params=pltpu.CompilerParams(dimension_semantics=("parallel",), vmem_limit_bytes=VMEM_BIG),
        name="inproj",
    )(*h_parts, mod_l, gpre, wl["w_in"], wl["g_q"], wl["w_qa"], wl["w_qb"], wl["g_kv"], wl["w_uk"], wl["w_uv"],
      rope_cos, rope_sin, wl["g_v"], wl["b_v"], wl["w_sp"], wl["b_sp"])


def _pair_out(acc0, acc1, lane):
    return jnp.where(lane < VD, acc0 / pltpu.roll(acc0, VD, 1), pltpu.roll(acc1, VD, 1) / acc1)


def _mla_kernel(q_ref, kx_ref, kc_ref, vx_ref, vc_ref, o_ref, s_scr, *, n_qx, with_ctx):
    lane = lax.broadcasted_iota(jnp.int32, (TQ, LANE), 1)

    def step(with_x):
        n_k = C + (S if with_x else 0)
        for hh in range(MLA_HPS):
            q = q_ref[hh]
            s_scr[hh, :, 0:C] = _dot_nt(q, kc_ref[hh])
            if with_x:
                s_scr[hh, :, C:] = _dot_nt(q, kx_ref[hh])
        accs = []
        for hh in range(MLA_HPS):
            m = jnp.max(s_scr[hh, :, 0:n_k], axis=-1, keepdims=True)
            acc = _dot(jnp.exp2(s_scr[hh, :, 0:C] - m).astype(BF16), vc_ref[hh])
            if with_x:
                acc = acc + _dot(jnp.exp2(s_scr[hh, :, C:] - m).astype(BF16), vx_ref[hh])
            accs.append(acc)
        for p in range(MLA_HPS // 2):
            o_ref[:, p * LANE:(p + 1) * LANE] = _pair_out(accs[2 * p], accs[2 * p + 1], lane).astype(o_ref.dtype)

    if not with_ctx:
        step(True)
        return
    is_x = pl.program_id(2) < n_qx
    pl.when(is_x)(lambda: step(True))
    pl.when(jnp.logical_not(is_x))(lambda: step(False))


def _mla_attention(q, k, v, with_ctx):
    n_qx = S // TQ
    n_q = n_qx + (1 if with_ctx else 0)
    xq = NX // TQ
    qrow = lambda b, q_: jnp.where(q_ < n_qx, b * n_qx + q_, xq + b)
    hps = MLA_HPS
    xkeys = pl.BlockSpec((hps, S, LANE), lambda b, p, q_: (p, b, 0))
    ckeys = pl.BlockSpec((hps, C, LANE), lambda b, p, q_: (p, NX // C + b, 0))
    return pl.pallas_call(
        functools.partial(_mla_kernel, n_qx=n_qx, with_ctx=with_ctx),
        grid=(B, H_MLA // hps, n_q),
        in_specs=[pl.BlockSpec((hps, TQ, LANE), lambda b, p, q_: (p, qrow(b, q_), 0)), xkeys, ckeys, xkeys, ckeys],
        out_specs=pl.BlockSpec((TQ, hps * VD), lambda b, p, q_: (qrow(b, q_), p)),
        out_shape=jax.ShapeDtypeStruct((N if with_ctx else NX, H_MLA * VD), BF16),
        scratch_shapes=[pltpu.VMEM((hps, TQ, C + S), F32)],
        compiler_params=pltpu.CompilerParams(dimension_semantics=("parallel", "parallel", "arbitrary"),
                                             vmem_limit_bytes=VMEM_BIG),
        name="mla_attention",
    )(q, k, k, v, v)


def _na_kernel(q_ref, kx_ref, kc_ref, vx_ref, vc_ref, bias_ref, o_ref, s_scr, *, n_g, with_ctx):
    lane = lax.broadcasted_iota(jnp.int32, (TQ, LANE), 1)

    def step(windowed):
        n_w = NA_KROWS * GRID_W
        n_k = C + (n_w if windowed else 0)
        if windowed:
            base = jnp.clip(pl.program_id(1) * NA_QROWS - WIN_R // 2, 0, ROWS - NA_KROWS)
            ks = pl.ds(pl.multiple_of(base * GRID_W, GRID_W), n_w)
        for hh in range(H_NA):
            p, half = divmod(hh, 2)
            qp = q_ref[p]
            q = jnp.where((lane < HD_NA) == (half == 0), qp, jnp.zeros_like(qp))
            s_scr[hh, :, 0:C] = _dot_nt(q, kc_ref[p])
            if windowed:
                s_scr[hh, :, C:] = _dot_nt(q, kx_ref[p, ks, :]) + bias_ref[0, hh]
        accs = []
        for hh in range(H_NA):
            m = jnp.max(s_scr[hh, :, 0:n_k], axis=-1, keepdims=True)
            acc = _dot(jnp.exp2(s_scr[hh, :, 0:C] - m).astype(BF16), vc_ref[hh])
            if windowed:
                acc = acc + _dot(jnp.exp2(s_scr[hh, :, C:] - m).astype(BF16), vx_ref[hh, ks, :])
            accs.append(acc)
        for p in range(H_NA // 2):
            o_ref[:, p * LANE:(p + 1) * LANE] = _pair_out(accs[2 * p], accs[2 * p + 1], lane).astype(o_ref.dtype)

    if not with_ctx:
        step(True)
        return
    is_x = pl.program_id(1) < n_g
    pl.when(is_x)(lambda: step(True))
    pl.when(jnp.logical_not(is_x))(lambda: step(False))


def _na_attention(q, k, v, bias, with_ctx):
    n_g = ROWS // NA_QROWS
    n_steps = n_g + (1 if with_ctx else 0)
    xq = NX // TQ
    qrow = lambda b, g: jnp.where(g < n_g, b * n_g + g, xq + b)
    kind = lambda b, g: (jnp.where(g == 0, 0, jnp.where(g >= n_g - 1, 2, 1)), 0, 0, 0)
    hp = H_NA // 2
    return pl.pallas_call(
        functools.partial(_na_kernel, n_g=n_g, with_ctx=with_ctx),
        grid=(B, n_steps),
        in_specs=[
            pl.BlockSpec((hp, TQ, LANE), lambda b, g: (0, qrow(b, g), 0)),
            pl.BlockSpec((hp, S, LANE), lambda b, g: (0, b, 0)),
            pl.BlockSpec((hp, C, LANE), lambda b, g: (0, NX // C + b, 0)),
            pl.BlockSpec((H_NA, S, LANE), lambda b, g: (0, b, 0)),
            pl.BlockSpec((H_NA, C, LANE), lambda b, g: (0, NX // C + b, 0)),
            pl.BlockSpec((1, H_NA, TQ, NA_KROWS * GRID_W), kind),
        ],
        out_specs=pl.BlockSpec((TQ, NA_W), lambda b, g: (qrow(b, g), 0)),
        out_shape=jax.ShapeDtypeStruct((N if with_ctx else NX, NA_W), BF16),
        scratch_shapes=[pltpu.VMEM((H_NA, TQ, C + NA_KROWS * GRID_W), F32)],
        compiler_params=pltpu.CompilerParams(dimension_semantics=("parallel", "arbitrary"),
                                             vmem_limit_bytes=VMEM_BIG),
        name="na_attention",
    )(q, k, k, v, v, bias)


def _na_bias_table(rpb_l):
    qc = np.arange(GRID_W)
    kc = np.arange(GRID_W)
    ws = np.clip(qc - WIN_C // 2, 0, GRID_W - WIN_C)
    col_ok = (kc[None, :] >= ws[:, None]) & (kc[None, :] < ws[:, None] + WIN_C)
    dc = np.clip(kc[None, :] - qc[:, None] + WIN_C - 1, 0, 2 * WIN_C - 2)
    sel_c = (dc[None] == np.arange(2 * WIN_C - 1)[:, None, None]).astype(np.float32)
    n_g = ROWS // NA_QROWS
    sel_r, oks = [], []
    for g in (0, 1, n_g - 1):
        base = int(np.clip(g * NA_QROWS - WIN_R // 2, 0, ROWS - NA_KROWS))
        r = g * NA_QROWS + np.arange(NA_QROWS)
        rs = np.clip(r - WIN_R // 2, 0, ROWS - WIN_R)
        kr = base + np.arange(NA_KROWS)
        row_ok = (kr[None, :] >= rs[:, None]) & (kr[None, :] < rs[:, None] + WIN_R)
        ri = kr[None, :] - r[:, None] + WIN_R - 1
        sel_r.append(((ri[:, :, None] == np.arange(2 * WIN_R - 1)) & row_ok[:, :, None]).astype(np.float32))
        oks.append(row_ok[:, None, :, None] & col_ok[None, :, None, :])
    hi = lax.Precision.HIGHEST
    by_col = jnp.einsum("hrd,dqk->hrqk", rpb_l * LOG2E, jnp.asarray(sel_c), precision=hi)
    bias = jnp.einsum("glar,hrqk->ghlqak", jnp.asarray(np.stack(sel_r)), by_col, precision=hi)
    bias = jnp.where(jnp.asarray(np.stack(oks))[:, None], bias, NEG)
    return bias.reshape(3, H_NA, TQ, NA_KROWS * GRID_W).astype(F32)


def _merge_core(ym_ref, yn_ref, yc_ref, sg_ref, ha_ref, hb_ref, mod_ref, gpost_ref, gffn_ref, wbr_ref, wout_ref):
    o1, o2 = H_MLA * VD, H_MLA * VD + NA_W
    z = (sg_ref[:, 0:D].astype(F32) * _dot(ym_ref[...], wbr_ref[0:o1, :])
         + sg_ref[:, D:2 * D].astype(F32) * _dot(yn_ref[...], wbr_ref[o1:o2, :])
         + sg_ref[:, 2 * D:3 * D].astype(F32) * _dot(yc_ref[...], wbr_ref[o2:, :]))
    o = _dot(z.astype(BF16), wout_ref[...])
    h1 = _stream_tile(ha_ref, hb_ref) + mod_ref[0, 2:3, :] * _rms(o, gpost_ref[...])
    t = _rms(h1, gffn_ref[...]) * (1.0 + mod_ref[0, 4:5, :]) + mod_ref[0, 3:4, :]
    return h1, t


def _merge_ffn_kernel(ym_ref, yn_ref, yc_ref, sg_ref, ha_ref, hb_ref, mod_ref, gpost_ref, gffn_ref, wbr_ref,
                      wout_ref, gpf_ref, w1_ref, w3_ref, w2_ref, o_ref):
    h1, t = _merge_core(ym_ref, yn_ref, yc_ref, sg_ref, ha_ref, hb_ref, mod_ref, gpost_ref, gffn_ref, wbr_ref,
                        wout_ref)
    tb = t.astype(BF16)
    half = D_FF // 2
    f = None
    for j in range(2):
        sl = slice(j * half, (j + 1) * half)
        a = _dot(tb, w1_ref[:, sl])
        b = _dot(tb, w3_ref[:, sl])
        part = _dot((a * jax.nn.sigmoid(a) * b).astype(BF16), w2_ref[sl, :])
        f = part if f is None else f + part
    o_ref[...] = h1 + mod_ref[0, 5:6, :] * _rms(f, gpf_ref[...])


def _merge_route_kernel(ym_ref, yn_ref, yc_ref, sg_ref, ha_ref, hb_ref, mod_ref, gpost_ref, gffn_ref, wbr_ref,
                        wout_ref, wr_ref, h1_ref, t_ref, e_ref, w_ref, cnt_ref):
    h1, t = _merge_core(ym_ref, yn_ref, yc_ref, sg_ref, ha_ref, hb_ref, mod_ref, gpost_ref, gffn_ref, wbr_ref,
                        wout_ref)
    h1_ref[...] = h1
    t_ref[...] = t
    t_hi = t.astype(BF16)
    t_lo = (t - t_hi.astype(F32)).astype(BF16)
    hi = _dot(t_hi, wr_ref[...])
    logits = hi[:, :LANE] + hi[:, LANE:] + _dot(t_lo, wr_ref[:, :LANE])
    lane = lax.broadcasted_iota(jnp.int32, logits.shape, 1)
    logits = jnp.where(lane < N_EXP, logits, -jnp.inf)
    l1 = jnp.max(logits, axis=-1, keepdims=True)
    e1 = jnp.min(jnp.where(logits == l1, lane, LANE), axis=-1, keepdims=True)
    rest = jnp.where(lane == e1, -jnp.inf, logits)
    l2 = jnp.max(rest, axis=-1, keepdims=True)
    e2 = jnp.min(jnp.where(rest == l2, lane, LANE), axis=-1, keepdims=True)
    ex = jnp.exp(l2 - l1)
    den = 1.0 + ex
    w_ref[...] = jnp.where(lane == 0, 1.0 / den, jnp.where(lane == 1, ex / den, 0.0))

    @pl.when(pl.program_id(0) == 0)
    def _():
        cnt_ref[...] = jnp.zeros(cnt_ref.shape, F32)

    oh1 = lane == e1
    oh2 = lane == e2
    both = jnp.where(oh1 | oh2, 1.0, 0.0)
    r_i = lax.broadcasted_iota(jnp.int32, (TM, TM), 0)
    c_i = lax.broadcasted_iota(jnp.int32, (TM, TM), 1)
    tri = jnp.where(c_i < r_i, 1.0, 0.0).astype(BF16)
    run = cnt_ref[0:1, :]
    before = _dot(tri, both.astype(BF16)) + run
    rank1 = jnp.sum(jnp.where(oh1, before, 0.0), axis=-1, keepdims=True).astype(jnp.int32)
    rank2 = jnp.sum(jnp.where(oh2, before, 0.0), axis=-1, keepdims=True).astype(jnp.int32)
    e_ref[...] = jnp.where(lane == 0, e1, jnp.where(lane == 1, e2, jnp.where(lane == 2, rank1,
                                                                               jnp.where(lane == 3, rank2, 0))))
    cnt_ref[...] = jnp.broadcast_to(run + jnp.sum(both, axis=0, keepdims=True), cnt_ref.shape)


def _merge_specs(h_parts):
    row = lambda i: (i, 0)
    return [
        pl.BlockSpec((TM, H_MLA * VD), row),
        pl.BlockSpec((TM, NA_W), row),
        pl.BlockSpec((TM, CM_W), row),
        pl.BlockSpec((TM, 3 * D), row),
    ] + _stream_specs(h_parts[0] is not h_parts[1]) + [
        pl.BlockSpec((1, N_MOD, D), _mod_index(S // TM)),
        _const_spec((1, D)),
        _const_spec((1, D)),
        _const_spec((D, D)),
        _const_spec((D, D)),
    ]


def _merge_ffn(ym, yn, yc, sg, h_parts, mod_l, gpost, gffn, wl, gpf, w1, w3, w2):
    row = lambda i: (i, 0)
    return pl.pallas_call(
        _merge_ffn_kernel,
        grid=(N // TM,),
        in_specs=_merge_specs(h_parts) + [_const_spec((1, D)), _const_spec((D, D_FF)), _const_spec((D, D_FF)),
                                          _const_spec((D_FF, D))],
        out_specs=pl.BlockSpec((TM, D), row),
        out_shape=jax.ShapeDtypeStruct((N, D), F32),
        compiler_params=pltpu.CompilerParams(dimension_semantics=("parallel",), vmem_limit_bytes=VMEM_BIG),
        name="merge_ffn",
    )(ym, yn, yc, sg, *h_parts, mod_l, gpost, gffn, wl["w_br"], wl["w_out"], gpf, w1, w3, w2)


def _merge_route(ym, yn, yc, sg, h_parts, mod_l, gpost, gffn, wl, w_router):
    row = lambda i: (i, 0)
    rows = lambda width, dtype: (jax.ShapeDtypeStruct((NX, width), dtype), pl.BlockSpec((TM, width), row))
    outs = [rows(D, F32), rows(D, F32), rows(LANE, jnp.int32), rows(LANE, F32),
            (jax.ShapeDtypeStruct((8, LANE), F32), pl.BlockSpec((8, LANE), lambda i: (0, 0)))]
    return pl.pallas_call(
        _merge_route_kernel,
        grid=(NX // TM,),
        in_specs=_merge_specs(h_parts) + [_const_spec((D, 2 * LANE))],
        out_specs=[o[1] for o in outs],
        out_shape=[o[0] for o in outs],
        compiler_params=pltpu.CompilerParams(dimension_semantics=("arbitrary",), vmem_limit_bytes=VMEM_BIG),
        name="merge_route",
    )(ym, yn, yc, sg, *h_parts, mod_l, gpost, gffn, wl["w_br"], wl["w_out"], w_router)


def _moe_kernel(be_ref, nu_ref, x_ref, w1_hbm, w3_hbm, w2_hbm, y_ref, wb1, wb3, wb2, st_in, st_out, sem):
    j = pl.program_id(0)
    n_f = D_FFE // FT
    e = be_ref[j]
    fresh = jnp.logical_or(pl.program_id(0) == 0, e != be_ref[jnp.maximum(j - 1, 0)])
    used = j < nu_ref[0]

    def ff_chunk(x, f):
        sl = slice(f * FT, (f + 1) * FT)
        a = _dot(x, wb1[:, sl])
        b = _dot(x, wb3[:, sl])
        part = _dot((a * jax.nn.sigmoid(a) * b).astype(BF16), wb2[sl, :])
        if f == 0:
            y_ref[...] = part
        else:
            y_ref[...] += part

    @pl.when(jnp.logical_and(fresh, used))
    def _():
        def chunk_copy(k):
            f, mat = divmod(k, 3)
            slot = k % 2
            cols = pl.ds(f * FT, FT)
            if mat == 2:
                return pltpu.make_async_copy(w2_hbm.at[e, cols, :], st_out.at[slot], sem.at[slot])
            return pltpu.make_async_copy((w1_hbm, w3_hbm)[mat].at[e, :, cols], st_in.at[slot], sem.at[slot])

        x = x_ref[...].astype(BF16)
        chunk_copy(0).start()
        for k in range(3 * n_f):
            if k + 1 < 3 * n_f:
                chunk_copy(k + 1).start()
            chunk_copy(k).wait()
            f, mat = divmod(k, 3)
            slot = k % 2
            sl = slice(f * FT, (f + 1) * FT)
            if mat == 0:
                wb1[:, sl] = st_in[slot].astype(BF16)
            elif mat == 1:
                wb3[:, sl] = st_in[slot].astype(BF16)
            else:
                wb2[sl, :] = st_out[slot].astype(BF16)
                ff_chunk(x, f)

    @pl.when(jnp.logical_and(jnp.logical_not(fresh), used))
    def _():
        x = x_ref[...].astype(BF16)
        for f in range(n_f):
            ff_chunk(x, f)

    @pl.when(jnp.logical_not(used))
    def _():
        y_ref[...] = jnp.zeros(y_ref.shape, y_ref.dtype)


def _moe_experts(x_pad, blk_expert, n_used, w1, w3, w2):
    hbm = pl.BlockSpec(memory_space=pl.ANY)
    grid_spec = pltpu.PrefetchScalarGridSpec(
        num_scalar_prefetch=2,
        grid=(N_BLK,),
        in_specs=[pl.BlockSpec((MOE_BLK, D), lambda j, be, nu: (j, 0)), hbm, hbm, hbm],
        out_specs=pl.BlockSpec((MOE_BLK, D), lambda j, be, nu: (j, 0)),
        scratch_shapes=[
            pltpu.VMEM((D, D_FFE), BF16), pltpu.VMEM((D, D_FFE), BF16), pltpu.VMEM((D_FFE, D), BF16),
            pltpu.VMEM((2, D, FT), F32), pltpu.VMEM((2, FT, D), F32), pltpu.SemaphoreType.DMA((2,)),
        ],
    )
    return pl.pallas_call(
        _moe_kernel,
        grid_spec=grid_spec,
        out_shape=jax.ShapeDtypeStruct((N_PAD, D), F32),
        compiler_params=pltpu.CompilerParams(dimension_semantics=("arbitrary",), vmem_limit_bytes=VMEM_BIG),
        name="moe_experts",
    )(blk_expert, n_used, x_pad, w1, w3, w2)


def _combine_kernel(ya_ref, yb_ref, w_ref, h1_ref, mod_ref, g_ref, o_ref):
    f = w_ref[:, 0:1] * ya_ref[...] + w_ref[:, 1:2] * yb_ref[...]
    o_ref[...] = h1_ref[...] + mod_ref[0, 5:6, :] * _rms(f, g_ref[...])


def _moe_combine(ya, yb, w, h1, mod_l, g):
    row = lambda i: (i, 0)
    return pl.pallas_call(
        _combine_kernel,
        grid=(NX // TM,),
        in_specs=[
            pl.BlockSpec((TM, D), row),
            pl.BlockSpec((TM, D), row),
            pl.BlockSpec((TM, LANE), row),
            pl.BlockSpec((TM, D), row),
            pl.BlockSpec((1, N_MOD, D), _mod_index(S // TM)),
            _const_spec((1, D)),
        ],
        out_specs=pl.BlockSpec((TM, D), row),
        out_shape=jax.ShapeDtypeStruct((NX, D), F32),
        compiler_params=pltpu.CompilerParams(dimension_semantics=("parallel",), vmem_limit_bytes=VMEM_BIG),
        name="moe_combine",
    )(ya, yb, w, h1, mod_l, g)


def _moe_layout(e, rank, counts):
    padded = (counts + MOE_BLK - 1) // MOE_BLK * MOE_BLK
    pad_end = jnp.cumsum(padded)
    pad_start = pad_end - padded
    onehot = e[:, :, None] == jnp.arange(N_EXP, dtype=jnp.int32)
    dest = jnp.sum(jnp.where(onehot, pad_start, 0), axis=-1) + rank
    blk_start = jnp.arange(N_BLK, dtype=jnp.int32) * MOE_BLK
    blk_expert = jnp.minimum(jnp.sum((blk_start[:, None] >= pad_end[None, :]).astype(jnp.int32), axis=1), N_EXP - 1)
    n_used = (pad_end[-1] // MOE_BLK).astype(jnp.int32).reshape(1)
    _, tok_sorted = lax.sort_key_val(dest.reshape(N_ASG), jnp.arange(N_ASG, dtype=jnp.int32) // 2)
    squeeze = pad_start - (jnp.cumsum(counts) - counts)
    slot = jnp.arange(N_PAD, dtype=jnp.int32)
    compact = slot - jnp.repeat(squeeze[blk_expert], MOE_BLK)
    slot_tok = tok_sorted.at[jnp.clip(compact, 0, N_ASG - 1)].get(mode="promise_in_bounds")
    return dest.astype(jnp.int32), slot_tok, blk_expert.astype(jnp.int32), n_used


def _rope_rot(w):
    half, quarter = ROPE // 2, ROPE // 4
    return jnp.concatenate([-w[..., quarter:half], w[..., :quarter], -w[..., half + quarter:], w[..., half:half + quarter]],
                           axis=-1)


def _rope_tables():
    half = ROPE // 2
    inv = ROPE_BASE ** (-(jnp.arange(0, half, 2, dtype=F32) / half))
    t = jnp.arange(S)
    ang_r = (t // GRID_W).astype(F32)[:, None] * inv
    ang_c = (t % GRID_W).astype(F32)[:, None] * inv
    ang = jnp.concatenate([ang_r, ang_r, ang_c, ang_c], axis=-1)
    cos32, sin32 = jnp.cos(ang), jnp.sin(ang)
    ones = lambda n: jnp.ones((S, n), F32)
    zeros = lambda n: jnp.zeros((S, n), F32)
    cos = jnp.concatenate([ones(NOPE), cos32, ones(LANE - QK)], axis=1)
    sin = jnp.concatenate([zeros(NOPE), sin32, zeros(LANE - QK)], axis=1)
    return (jnp.concatenate([cos, jnp.ones((TM, LANE), F32)], axis=0),
            jnp.concatenate([sin, jnp.zeros((TM, LANE), F32)], axis=0))


def _pad_heads(w, n_heads, width):
    w = w.reshape(w.shape[0], n_heads, width)
    return jnp.concatenate([w, jnp.zeros((w.shape[0], n_heads, LANE - width), w.dtype)], axis=2).reshape(w.shape[0], -1)


def _pack_layer(l, w_in, g_q, w_uq, g_kv, w_ukv, g_v, b_v, w_sp, b_sp, w_br, w_out):
    wi = w_in[l]
    offs = np.cumsum((0, Q_LORA, KV_LORA, ROPE, NA_W, NA_W, NA_W, CM_W, CM_W))
    cq, ckv, kr, naq, nak, nav, u, v = (wi[:, offs[i]:offs[i + 1]] for i in range(8))
    gl = wi[:, offs[8]:]
    zl = jnp.zeros((D, NOPE), F32)
    zr = jnp.zeros((D, LANE - NOPE - ROPE), F32)
    w_in_p = jnp.concatenate([cq, ckv, zl, kr, zr, zl, _rope_rot(kr), zr, naq, nak, _pad_heads(nav, H_NA, HD_NA),
                              u, v, gl], axis=1).astype(BF16)
    uq = w_uq[l].reshape(Q_LORA, H_MLA, QK)
    zq = jnp.zeros((Q_LORA, H_MLA, LANE - QK), F32)
    w_qa = jnp.concatenate([uq, zq], axis=2).reshape(Q_LORA, H_MLA * LANE).astype(BF16)
    w_qb = jnp.concatenate([jnp.zeros((Q_LORA, H_MLA, NOPE), F32), _rope_rot(uq[:, :, NOPE:]), zq], axis=2)
    w_qb = w_qb.reshape(Q_LORA, H_MLA * LANE).astype(BF16)
    ukv = w_ukv[l].reshape(KV_LORA, H_MLA, NOPE + VD)
    w_uk = _pad_heads(ukv[:, :, :NOPE].reshape(KV_LORA, -1), H_MLA, NOPE).astype(BF16)
    w_uv = _pad_heads(ukv[:, :, NOPE:].reshape(KV_LORA, -1), H_MLA, VD).astype(BF16)
    b_sp_t = jnp.repeat(b_sp[l].T, CM_W // CM_G, axis=1)
    return dict(
        w_in=w_in_p, g_q=g_q[l].reshape(1, -1), w_qa=w_qa, w_qb=w_qb, g_kv=g_kv[l].reshape(1, -1), w_uk=w_uk,
        w_uv=w_uv, g_v=g_v[l].reshape(1, -1), b_v=b_v[l].reshape(1, -1), w_sp=w_sp[l].astype(BF16), b_sp=b_sp_t,
        w_br=w_br[l].astype(BF16), w_out=w_out[l].astype(BF16))


def kernel(x, c, ctx, c_ctx, w_mod, b_mod, g_pre_mix, g_post_mix, g_pre_ffn, g_post_ffn, w_in, g_q, w_uq, g_kv, w_ukv, rpb, g_v, b_v, w_sp, b_sp, w_br, w_out, w_ffn1, w_ffn3, w_ffn2, w_router, w_moe1, w_moe3, w_moe2):
    assert DEPTH == 2, "layer 0 is the dense layer with context outputs, layer 1 the last (expert) layer"
    h_parts = (x.reshape(NX, D), ctx.reshape(NC, D))
    c_all = jnp.concatenate([c, c_ctx[None, :], jnp.zeros((16 - B - 1, D), F32)], axis=0)
    mod = _modulation(c_all, w_mod, b_mod).reshape(DEPTH, 16, N_MOD, D)
    rope_cos, rope_sin = _rope_tables()
    row1 = lambda a: a.reshape(1, -1)

    for l in range(DEPTH):
        last = l == DEPTH - 1
        wl = _pack_layer(l, w_in, g_q, w_uq, g_kv, w_ukv, g_v, b_v, w_sp, b_sp, w_br, w_out)
        mod_l = mod[l]
        q, k, v, naq, nak, nav, ycm, sg = _inproj(h_parts, mod_l, row1(g_pre_mix[l]), wl, rope_cos, rope_sin)
        ym = _mla_attention(q, k, v, with_ctx=not last)
        yn = _na_attention(naq, nak, nav, _na_bias_table(rpb[l]), with_ctx=not last)
        i = l // 2
        if not last:
            h = _merge_ffn(ym, yn, ycm, sg, h_parts, mod_l, row1(g_post_mix[l]), row1(g_pre_ffn[l]), wl,
                           row1(g_post_ffn[l]), w_ffn1[i].astype(BF16), w_ffn3[i].astype(BF16),
                           w_ffn2[i].astype(BF16))
            h_parts = (h, h)
        else:
            wr_hi = w_router[i].astype(BF16)
            wr_lo = (w_router[i] - wr_hi.astype(F32)).astype(BF16)
            zpad = jnp.zeros((D, LANE - N_EXP), BF16)
            wr = jnp.concatenate([wr_hi, zpad, wr_lo, zpad], axis=1)
            h1, t, e, w, cnt = _merge_route(ym, yn, ycm, sg, h_parts, mod_l, row1(g_post_mix[l]),
                                            row1(g_pre_ffn[l]), wl, wr)
            dest, slot_tok, blk_expert, n_used = _moe_layout(e[:, 0:2], e[:, 2:4], cnt[0, :N_EXP].astype(jnp.int32))
            inb = dict(mode="promise_in_bounds")
            x_pad = t.at[slot_tok].get(**inb)
            y = _moe_experts(x_pad, blk_expert, n_used, w_moe1[i], w_moe3[i], w_moe2[i])
            h = _moe_combine(y.at[dest[:, 0]].get(**inb), y.at[dest[:, 1]].get(**inb), w, h1, mod_l,
                             row1(g_post_ffn[l]))
    return h[:NX].reshape(B, S, D)
```

```python
import functools

import numpy as np
import jax
import jax.numpy as jnp
from jax import lax
from jax.experimental import pallas as pl
from jax.experimental.pallas import tpu as pltpu

F32 = jnp.float32
BF16 = jnp.bfloat16

D = 1024
B = 8
S = 4096
DEPTH = 2
GRID_W = 64
ROWS = S // GRID_W
C = 256
EPS = 1e-6
NEG = -1e30
LOG2E = 1.4426950408889634
N_MOD = 6

H_MLA = 8
Q_LORA = 384
KV_LORA = 256
NOPE = 64
ROPE = 32
VD = 64
QK = NOPE + ROPE
ROPE_BASE = 10000.0

H_NA = 4
HD_NA = 64
NA_W = H_NA * HD_NA
WIN_R = 8
WIN_C = 16
NA_QROWS = 4
NA_KROWS = 12

CM_G = 4
CM_CHUNK = 128
CM_W = 256

D_FF = 2816
N_EXP = 8
D_FFE = 3584
MOE_BLK = 512

NX = B * S
NC = B * C
N = NX + NC
LANE = 128

TM = 512
TQ = 256
MLA_HPS = 4
FT = 512
N_ASG = NX * 2
N_BLK = N_ASG // MOE_BLK + N_EXP
N_PAD = N_BLK * MOE_BLK

O_CQ, O_CKV, O_KRA, O_KRB, O_NAQ, O_NAK, O_NAV, O_U, O_V, O_GL = 0, 384, 640, 768, 896, 1152, 1408, 1920, 2176, 2432
W_IN_PACKED = O_GL + 3 * D

VMEM_BIG = 56 * 1024 * 1024


def _rms(xf, g):
    return xf * lax.rsqrt(jnp.mean(xf * xf, axis=-1, keepdims=True) + EPS) * g


def _dot(a, b):
    return jnp.dot(a, b, preferred_element_type=F32)


def _dot_nt(a, b):
    return lax.dot_general(a, b, (((1,), (1,)), ((), ())), preferred_element_type=F32)


def _const_spec(shape):
    nd = len(shape)
    return pl.BlockSpec(shape, lambda *_: (0,) * nd, pipeline_mode=pl.Buffered(1))


def _mod_index(tiles_per_batch):
    return lambda i: (jnp.minimum(i // tiles_per_batch, B), 0, 0)


def _stream_specs(split):
    nxt = NX // TM
    off = nxt if split else 0
    return [pl.BlockSpec((TM, D), lambda i: (jnp.minimum(i, nxt - 1), 0)),
            pl.BlockSpec((TM, D), lambda i: (jnp.maximum(i, nxt) - off, 0))]


def _stream_tile(ha_ref, hb_ref):
    return jnp.where(pl.program_id(0) < NX // TM, ha_ref[...], hb_ref[...])


def _mod_kernel(c_ref, w_ref, b_ref, o_ref):
    c = c_ref[...]
    sc = c * jax.nn.sigmoid(c)
    o_ref[0] = jnp.dot(sc, w_ref[0], preferred_element_type=F32, precision=lax.Precision.HIGHEST) + b_ref[0]


def _modulation(c_all, w_mod, b_mod):
    tn = 1536
    return pl.pallas_call(
        _mod_kernel,
        grid=(DEPTH, N_MOD * D // tn),
        in_specs=[
            pl.BlockSpec((16, D), lambda l, j: (0, 0)),
            pl.BlockSpec((1, D, tn), lambda l, j: (l, 0, j)),
            pl.BlockSpec((1, 1, tn), lambda l, j: (l, 0, j)),
        ],
        out_specs=pl.BlockSpec((1, 16, tn), lambda l, j: (l, 0, j)),
        out_shape=jax.ShapeDtypeStruct((DEPTH, 16, N_MOD * D), F32),
        compiler_params=pltpu.CompilerParams(dimension_semantics=("parallel", "parallel"), vmem_limit_bytes=VMEM_BIG),
        name="modulation",
    )(c_all, w_mod, b_mod.reshape(DEPTH, 1, N_MOD * D))


def _inproj_kernel(ha_ref, hb_ref, mod_ref, gpre_ref, win_ref, gq_ref, wqa_ref, wqb_ref, gkv_ref, wuk_ref, wuv_ref,
                   cos_ref, sin_ref, gv_ref, bv_ref, wsp_ref, bsp_ref,
                   q_ref, k_ref, v_ref, naq_ref, nak_ref, nav_ref, ycm_ref, sg_ref):
    n = _rms(_stream_tile(ha_ref, hb_ref), gpre_ref[...]) * (1.0 + mod_ref[0, 1:2, :]) + mod_ref[0, 0:1, :]
    nb = n.astype(BF16)

    def proj(a, b):
        return _dot(nb, win_ref[:, a:b])

    cos = cos_ref[...]
    sin = sin_ref[...]
    ones_hi = jnp.where(lax.broadcasted_iota(jnp.int32, (TM, LANE), 1) >= VD, 1.0, 0.0)

    cqn = _rms(proj(O_CQ, O_CKV), gq_ref[...]).astype(BF16)
    qa = _dot(cqn, wqa_ref[...])
    qb = _dot(cqn, wqb_ref[...])
    for hh in range(H_MLA):
        sl = slice(hh * LANE, (hh + 1) * LANE)
        q_ref[hh] = ((qa[:, sl] * cos + qb[:, sl] * sin) * (QK ** -0.5 * LOG2E)).astype(BF16)
    ckvn = _rms(proj(O_CKV, O_KRA), gkv_ref[...]).astype(BF16)
    krp = proj(O_KRA, O_KRB) * cos + proj(O_KRB, O_NAQ) * sin
    kk = _dot(ckvn, wuk_ref[...])
    for hh in range(H_MLA):
        k_ref[hh] = (kk[:, hh * LANE:(hh + 1) * LANE] + krp).astype(BF16)
    vv = _dot(ckvn, wuv_ref[...])
    for hh in range(H_MLA):
        v_ref[hh] = (vv[:, hh * LANE:(hh + 1) * LANE] + ones_hi).astype(BF16)

    naq = proj(O_NAQ, O_NAK) * (HD_NA ** -0.5 * LOG2E)
    nak = proj(O_NAK, O_NAV)
    for p in range(H_NA // 2):
        sl = slice(p * LANE, (p + 1) * LANE)
        naq_ref[p] = naq[:, sl].astype(BF16)
        nak_ref[p] = nak[:, sl].astype(BF16)
    nav = proj(O_NAV, O_U)
    for hh in range(H_NA):
        nav_ref[hh] = (nav[:, hh * LANE:(hh + 1) * LANE] + ones_hi).astype(BF16)

    u = proj(O_U, O_V)
    v = proj(O_V, O_GL)
    mu = jnp.mean(v, axis=-1, keepdims=True)
    vc = v - mu
    vn = (vc * lax.rsqrt(jnp.mean(vc * vc, axis=-1, keepdims=True) + EPS) * gv_ref[...] + bv_ref[...]).astype(BF16)
    grp = lax.broadcasted_iota(jnp.int32, (CM_CHUNK, CM_W), 1) // (CM_W // CM_G)
    for ch in range(TM // CM_CHUNK):
        rs = slice(ch * CM_CHUNK, (ch + 1) * CM_CHUNK)
        vch = vn[rs, :]
        sp = _dot(wsp_ref[CM_G - 1], vch)
        for g in range(CM_G - 2, -1, -1):
            sp = jnp.where(grp == g, _dot(wsp_ref[g], vch), sp)
        ycm_ref[rs, :] = (u[rs, :] * (sp + bsp_ref[...])).astype(BF16)

    for j in range(3):
        sg_ref[:, j * D:(j + 1) * D] = jax.nn.sigmoid(proj(O_GL + j * D, O_GL + (j + 1) * D)).astype(BF16)


def _inproj(h_parts, mod_l, gpre, wl, rope_cos, rope_sin):
    tiles_x = S // TM
    n_tiles = N // TM
    pos_idx = lambda i: (jnp.where(i < NX // TM, i % tiles_x, tiles_x), 0)
    row = lambda i: (i, 0)
    hrow = lambda i: (0, i, 0)
    in_specs = _stream_specs(h_parts[0] is not h_parts[1]) + [
        pl.BlockSpec((1, N_MOD, D), _mod_index(tiles_x)),
        _const_spec((1, D)),
        _const_spec((D, W_IN_PACKED)),
        _const_spec((1, Q_LORA)),
        _const_spec((Q_LORA, H_MLA * LANE)),
        _const_spec((Q_LORA, H_MLA * LANE)),
        _const_spec((1, KV_LORA)),
        _const_spec((KV_LORA, H_MLA * LANE)),
        _const_spec((KV_LORA, H_MLA * LANE)),
        pl.BlockSpec((TM, LANE), pos_idx),
        pl.BlockSpec((TM, LANE), pos_idx),
        _const_spec((1, CM_W)),
        _const_spec((1, CM_W)),
        _const_spec((CM_G, CM_CHUNK, CM_CHUNK)),
        _const_spec((CM_CHUNK, CM_W)),
    ]
    heads = lambda n: (jax.ShapeDtypeStruct((n, N, LANE), BF16), pl.BlockSpec((n, TM, LANE), hrow))
    outs = [
        heads(H_MLA), heads(H_MLA), heads(H_MLA), heads(H_NA // 2), heads(H_NA // 2), heads(H_NA),
        (jax.ShapeDtypeStruct((N, CM_W), BF16), pl.BlockSpec((TM, CM_W), row)),
        (jax.ShapeDtypeStruct((N, 3 * D), BF16), pl.BlockSpec((TM, 3 * D), row)),
    ]
    return pl.pallas_call(
        _inproj_kernel,
        grid=(n_tiles,),
        in_specs=in_specs,
        out_specs=[o[1] for o in outs],
        out_shape=[o[0] for o in outs],
        compiler_params=pltpu.CompilerParams(dimension_semantics=("parallel",), vmem_limit_bytes=VMEM_BIG),
        name="inproj",
    )(*h_parts, mod_l, gpre, wl["w_in"], wl["g_q"], wl["w_qa"], wl["w_qb"], wl["g_kv"], wl["w_uk"], wl["w_uv"],
      rope_cos, rope_sin, wl["g_v"], wl["b_v"], wl["w_sp"], wl["b_sp"])


def _pair_out(acc0, acc1, lane):
    return jnp.where(lane < VD, acc0 / pltpu.roll(acc0, VD, 1), pltpu.roll(acc1, VD, 1) / acc1)


def _mla_kernel(q_ref, kx_ref, kc_ref, vx_ref, vc_ref, o_ref, s_scr, *, n_qx, with_ctx):
    lane = lax.broadcasted_iota(jnp.int32, (TQ, LANE), 1)

    def step(with_x):
        n_k = C + (S if with_x else 0)
        for hh in range(MLA_HPS):
            q = q_ref[hh]
            s_scr[hh, :, 0:C] = _dot_nt(q, kc_ref[hh])
            if with_x:
                s_scr[hh, :, C:] = _dot_nt(q, kx_ref[hh])
        accs = []
        for hh in range(MLA_HPS):
            m = jnp.max(s_scr[hh, :, 0:n_k], axis=-1, keepdims=True)
            acc = _dot(jnp.exp2(s_scr[hh, :, 0:C] - m).astype(BF16), vc_ref[hh])
            if with_x:
                acc = acc + _dot(jnp.exp2(s_scr[hh, :, C:] - m).astype(BF16), vx_ref[hh])
            accs.append(acc)
        for p in range(MLA_HPS // 2):
            o_ref[:, p * LANE:(p + 1) * LANE] = _pair_out(accs[2 * p], accs[2 * p + 1], lane).astype(o_ref.dtype)

    if not with_ctx:
        step(True)
        return
    is_x = pl.program_id(2) < n_qx
    pl.when(is_x)(lambda: step(True))
    pl.when(jnp.logical_not(is_x))(lambda: step(False))


def _mla_attention(q, k, v, with_ctx):
    n_qx = S // TQ
    n_q = n_qx + (1 if with_ctx else 0)
    xq = NX // TQ
    qrow = lambda b, q_: jnp.where(q_ < n_qx, b * n_qx + q_, xq + b)
    hps = MLA_HPS
    xkeys = pl.BlockSpec((hps, S, LANE), lambda b, p, q_: (p, b, 0))
    ckeys = pl.BlockSpec((hps, C, LANE), lambda b, p, q_: (p, NX // C + b, 0))
    return pl.pallas_call(
        functools.partial(_mla_kernel, n_qx=n_qx, with_ctx=with_ctx),
        grid=(B, H_MLA // hps, n_q),
        in_specs=[pl.BlockSpec((hps, TQ, LANE), lambda b, p, q_: (p, qrow(b, q_), 0)), xkeys, ckeys, xkeys, ckeys],
        out_specs=pl.BlockSpec((TQ, hps * VD), lambda b, p, q_: (qrow(b, q_), p)),
        out_shape=jax.ShapeDtypeStruct((N if with_ctx else NX, H_MLA * VD), BF16),
        scratch_shapes=[pltpu.VMEM((hps, TQ, C + S), F32)],
        compiler_params=pltpu.CompilerParams(dimension_semantics=("parallel", "parallel", "arbitrary"),
                                             vmem_limit_bytes=VMEM_BIG),
        name="mla_attention",
    )(q, k, k, v, v)


def _na_kernel(q_ref, kx_ref, kc_ref, vx_ref, vc_ref, bias_ref, o_ref, s_scr, *, n_g, with_ctx):
    lane = lax.broadcasted_iota(jnp.int32, (TQ, LANE), 1)

    def step(windowed):
        n_w = NA_KROWS * GRID_W
        n_k = C + (n_w if windowed else 0)
        if windowed:
            base = jnp.clip(pl.program_id(1) * NA_QROWS - WIN_R // 2, 0, ROWS - NA_KROWS)
            ks = pl.ds(pl.multiple_of(base * GRID_W, GRID_W), n_w)
        for hh in range(H_NA):
            p, half = divmod(hh, 2)
            qp = q_ref[p]
            q = jnp.where((lane < HD_NA) == (half == 0), qp, jnp.zeros_like(qp))
            s_scr[hh, :, 0:C] = _dot_nt(q, kc_ref[p])
            if windowed:
                s_scr[hh, :, C:] = _dot_nt(q, kx_ref[p, ks, :]) + bias_ref[0, hh]
        accs = []
        for hh in range(H_NA):
            m = jnp.max(s_scr[hh, :, 0:n_k], axis=-1, keepdims=True)
            acc = _dot(jnp.exp2(s_scr[hh, :, 0:C] - m).astype(BF16), vc_ref[hh])
            if windowed:
                acc = acc + _dot(jnp.exp2(s_scr[hh, :, C:] - m).astype(BF16), vx_ref[hh, ks, :])
            accs.append(acc)
        for p in range(H_NA // 2):
            o_ref[:, p * LANE:(p + 1) * LANE] = _pair_out(accs[2 * p], accs[2 * p + 1], lane).astype(o_ref.dtype)

    if not with_ctx:
        step(True)
        return
    is_x = pl.program_id(1) < n_g
    pl.when(is_x)(lambda: step(True))
    pl.when(jnp.logical_not(is_x))(lambda: step(False))


def _na_attention(q, k, v, bias, with_ctx):
    n_g = ROWS // NA_QROWS
    n_steps = n_g + (1 if with_ctx else 0)
    xq = NX // TQ
    qrow = lambda b, g: jnp.where(g < n_g, b * n_g + g, xq + b)
    kind = lambda b, g: (jnp.where(g == 0, 0, jnp.where(g >= n_g - 1, 2, 1)), 0, 0, 0)
    hp = H_NA // 2
    return pl.pallas_call(
        functools.partial(_na_kernel, n_g=n_g, with_ctx=with_ctx),
        grid=(B, n_steps),
        in_specs=[
            pl.BlockSpec((hp, TQ, LANE), lambda b, g: (0, qrow(b, g), 0)),
            pl.BlockSpec((hp, S, LANE), lambda b, g: (0, b, 0)),
            pl.BlockSpec((hp, C, LANE), lambda b, g: (0, NX // C + b, 0)),
            pl.BlockSpec((H_NA, S, LANE), lambda b, g: (0, b, 0)),
            pl.BlockSpec((H_NA, C, LANE), lambda b, g: (0, NX // C + b, 0)),
            pl.BlockSpec((1, H_NA, TQ, NA_KROWS * GRID_W), kind),
        ],
        out_specs=pl.BlockSpec((TQ, NA_W), lambda b, g: (qrow(b, g), 0)),
        out_shape=jax.ShapeDtypeStruct((N if with_ctx else NX, NA_W), BF16),
        scratch_shapes=[pltpu.VMEM((H_NA, TQ, C + NA_KROWS * GRID_W), F32)],
        compiler_params=pltpu.CompilerParams(dimension_semantics=("parallel", "arbitrary"),
                                             vmem_limit_bytes=VMEM_BIG),
        name="na_attention",
    )(q, k, k, v, v, bias)


def _na_bias_table(rpb_l):
    qc = np.arange(GRID_W)
    kc = np.arange(GRID_W)
    ws = np.clip(qc - WIN_C // 2, 0, GRID_W - WIN_C)
    col_ok = (kc[None, :] >= ws[:, None]) & (kc[None, :] < ws[:, None] + WIN_C)
    dc = np.clip(kc[None, :] - qc[:, None] + WIN_C - 1, 0, 2 * WIN_C - 2)
    sel_c = (dc[None] == np.arange(2 * WIN_C - 1)[:, None, None]).astype(np.float32)
    n_g = ROWS // NA_QROWS
    sel_r, oks = [], []
    for g in (0, 1, n_g - 1):
        base = int(np.clip(g * NA_QROWS - WIN_R // 2, 0, ROWS - NA_KROWS))
        r = g * NA_QROWS + np.arange(NA_QROWS)
        rs = np.clip(r - WIN_R // 2, 0, ROWS - WIN_R)
        kr = base + np.arange(NA_KROWS)
        row_ok = (kr[None, :] >= rs[:, None]) & (kr[None, :] < rs[:, None] + WIN_R)
        ri = kr[None, :] - r[:, None] + WIN_R - 1
        sel_r.append(((ri[:, :, None] == np.arange(2 * WIN_R - 1)) & row_ok[:, :, None]).astype(np.float32))
        oks.append(row_ok[:, None, :, None] & col_ok[None, :, None, :])
    hi = lax.Precision.HIGHEST
    by_col = jnp.einsum("hrd,dqk->hrqk", rpb_l * LOG2E, jnp.asarray(sel_c), precision=hi)
    bias = jnp.einsum("glar,hrqk->ghlqak", jnp.asarray(np.stack(sel_r)), by_col, precision=hi)
    bias = jnp.where(jnp.asarray(np.stack(oks))[:, None], bias, NEG)
    return bias.reshape(3, H_NA, TQ, NA_KROWS * GRID_W).astype(F32)


def _merge_core(ym_ref, yn_ref, yc_ref, sg_ref, ha_ref, hb_ref, mod_ref, gpost_ref, gffn_ref, wbr_ref, wout_ref):
    o1, o2 = H_MLA * VD, H_MLA * VD + NA_W
    z = (sg_ref[:, 0:D].astype(F32) * _dot(ym_ref[...], wbr_ref[0:o1, :])
         + sg_ref[:, D:2 * D].astype(F32) * _dot(yn_ref[...], wbr_ref[o1:o2, :])
         + sg_ref[:, 2 * D:3 * D].astype(F32) * _dot(yc_ref[...], wbr_ref[o2:, :]))
    o = _dot(z.astype(BF16), wout_ref[...])
    h1 = _stream_tile(ha_ref, hb_ref) + mod_ref[0, 2:3, :] * _rms(o, gpost_ref[...])
    t = _rms(h1, gffn_ref[...]) * (1.0 + mod_ref[0, 4:5, :]) + mod_ref[0, 3:4, :]
    return h1, t


def _merge_ffn_kernel(ym_ref, yn_ref, yc_ref, sg_ref, ha_ref, hb_ref, mod_ref, gpost_ref, gffn_ref, wbr_ref,
                      wout_ref, gpf_ref, w1_ref, w3_ref, w2_ref, o_ref):
    h1, t = _merge_core(ym_ref, yn_ref, yc_ref, sg_ref, ha_ref, hb_ref, mod_ref, gpost_ref, gffn_ref, wbr_ref,
                        wout_ref)
    tb = t.astype(BF16)
    half = D_FF // 2
    f = None
    for j in range(2):
        sl = slice(j * half, (j + 1) * half)
        a = _dot(tb, w1_ref[:, sl])
        b = _dot(tb, w3_ref[:, sl])
        part = _dot((a * jax.nn.sigmoid(a) * b).astype(BF16), w2_ref[sl, :])
        f = part if f is None else f + part
    o_ref[...] = h1 + mod_ref[0, 5:6, :] * _rms(f, gpf_ref[...])


def _merge_route_kernel(ym_ref, yn_ref, yc_ref, sg_ref, ha_ref, hb_ref, mod_ref, gpost_ref, gffn_ref, wbr_ref,
                        wout_ref, wr_ref, h1_ref, t_ref, e_ref, w_ref, cnt_ref):
    h1, t = _merge_core(ym_ref, yn_ref, yc_ref, sg_ref, ha_ref, hb_ref, mod_ref, gpost_ref, gffn_ref, wbr_ref,
                        wout_ref)
    h1_ref[...] = h1
    t_ref[...] = t
    t_hi = t.astype(BF16)
    t_lo = (t - t_hi.astype(F32)).astype(BF16)
    hi = _dot(t_hi, wr_ref[...])
    logits = hi[:, :LANE] + hi[:, LANE:] + _dot(t_lo, wr_ref[:, :LANE])
    lane = lax.broadcasted_iota(jnp.int32, logits.shape, 1)
    logits = jnp.where(lane < N_EXP, logits, -jnp.inf)
    l1 = jnp.max(logits, axis=-1, keepdims=True)
    e1 = jnp.min(jnp.where(logits == l1, lane, LANE), axis=-1, keepdims=True)
    rest = jnp.where(lane == e1, -jnp.inf, logits)
    l2 = jnp.max(rest, axis=-1, keepdims=True)
    e2 = jnp.min(jnp.where(rest == l2, lane, LANE), axis=-1, keepdims=True)
    ex = jnp.exp(l2 - l1)
    den = 1.0 + ex
    w_ref[...] = jnp.where(lane == 0, 1.0 / den, jnp.where(lane == 1, ex / den, 0.0))

    @pl.when(pl.program_id(0) == 0)
    def _():
        cnt_ref[...] = jnp.zeros(cnt_ref.shape, F32)

    oh1 = lane == e1
    oh2 = lane == e2
    both = jnp.where(oh1 | oh2, 1.0, 0.0)
    r_i = lax.broadcasted_iota(jnp.int32, (TM, TM), 0)
    c_i = lax.broadcasted_iota(jnp.int32, (TM, TM), 1)
    tri = jnp.where(c_i < r_i, 1.0, 0.0).astype(BF16)
    run = cnt_ref[0:1, :]
    before = _dot(tri, both.astype(BF16)) + run
    rank1 = jnp.sum(jnp.where(oh1, before, 0.0), axis=-1, keepdims=True).astype(jnp.int32)
    rank2 = jnp.sum(jnp.where(oh2, before, 0.0), axis=-1, keepdims=True).astype(jnp.int32)
    e_ref[...] = jnp.where(lane == 0, e1, jnp.where(lane == 1, e2, jnp.where(lane == 2, rank1,
                                                                               jnp.where(lane == 3, rank2, 0))))
    cnt_ref[...] = jnp.broadcast_to(run + jnp.sum(both, axis=0, keepdims=True), cnt_ref.shape)


def _merge_specs(h_parts):
    row = lambda i: (i, 0)
    return [
        pl.BlockSpec((TM, H_MLA * VD), row),
        pl.BlockSpec((TM, NA_W), row),
        pl.BlockSpec((TM, CM_W), row),
        pl.BlockSpec((TM, 3 * D), row),
    ] + _stream_specs(h_parts[0] is not h_parts[1]) + [
        pl.BlockSpec((1, N_MOD, D), _mod_index(S // TM)),
        _const_spec((1, D)),
        _const_spec((1, D)),
        _const_spec((D, D)),
        _const_spec((D, D)),
    ]


def _merge_ffn(ym, yn, yc, sg, h_parts, mod_l, gpost, gffn, wl, gpf, w1, w3, w2):
    row = lambda i: (i, 0)
    return pl.pallas_call(
        _merge_ffn_kernel,
        grid=(N // TM,),
        in_specs=_merge_specs(h_parts) + [_const_spec((1, D)), _const_spec((D, D_FF)), _const_spec((D, D_FF)),
                                          _const_spec((D_FF, D))],
        out_specs=pl.BlockSpec((TM, D), row),
        out_shape=jax.ShapeDtypeStruct((N, D), F32),
        compiler_params=pltpu.CompilerParams(dimension_semantics=("parallel",), vmem_limit_bytes=VMEM_BIG),
        name="merge_ffn",
    )(ym, yn, yc, sg, *h_parts, mod_l, gpost, gffn, wl["w_br"], wl["w_out"], gpf, w1, w3, w2)


def _merge_route(ym, yn, yc, sg, h_parts, mod_l, gpost, gffn, wl, w_router):
    row = lambda i: (i, 0)
    rows = lambda width, dtype: (jax.ShapeDtypeStruct((NX, width), dtype), pl.BlockSpec((TM, width), row))
    outs = [rows(D, F32), rows(D, F32), rows(LANE, jnp.int32), rows(LANE, F32),
            (jax.ShapeDtypeStruct((8, LANE), F32), pl.BlockSpec((8, LANE), lambda i: (0, 0)))]
    return pl.pallas_call(
        _merge_route_kernel,
        grid=(NX // TM,),
        in_specs=_merge_specs(h_parts) + [_const_spec((D, 2 * LANE))],
        out_specs=[o[1] for o in outs],
        out_shape=[o[0] for o in outs],
        compiler_params=pltpu.CompilerParams(dimension_semantics=("arbitrary",), vmem_limit_bytes=VMEM_BIG),
        name="merge_route",
    )(ym, yn, yc, sg, *h_parts, mod_l, gpost, gffn, wl["w_br"], wl["w_out"], w_router)


def _moe_kernel(be_ref, nu_ref, x_ref, w1_hbm, w3_hbm, w2_hbm, y_ref, wb1, wb3, wb2, st_in, st_out, sem):
    j = pl.program_id(0)
    n_f = D_FFE // FT
    e = be_ref[j]
    fresh = jnp.logical_or(pl.program_id(0) == 0, e != be_ref[jnp.maximum(j - 1, 0)])
    used = j < nu_ref[0]

    def ff_chunk(x, f):
        sl = slice(f * FT, (f + 1) * FT)
        a = _dot(x, wb1[:, sl])
        b = _dot(x, wb3[:, sl])
        part = _dot((a * jax.nn.sigmoid(a) * b).astype(BF16), wb2[sl, :])
        if f == 0:
            y_ref[...] = part
        else:
            y_ref[...] += part

    @pl.when(jnp.logical_and(fresh, used))
    def _():
        def chunk_copy(k):
            f, mat = divmod(k, 3)
            slot = k % 2
            cols = pl.ds(f * FT, FT)
            if mat == 2:
                return pltpu.make_async_copy(w2_hbm.at[e, cols, :], st_out.at[slot], sem.at[slot])
            return pltpu.make_async_copy((w1_hbm, w3_hbm)[mat].at[e, :, cols], st_in.at[slot], sem.at[slot])

        x = x_ref[...].astype(BF16)
        chunk_copy(0).start()
        for k in range(3 * n_f):
            if k + 1 < 3 * n_f:
                chunk_copy(k + 1).start()
            chunk_copy(k).wait()
            f, mat = divmod(k, 3)
            slot = k % 2
            sl = slice(f * FT, (f + 1) * FT)
            if mat == 0:
                wb1[:, sl] = st_in[slot].astype(BF16)
            elif mat == 1:
                wb3[:, sl] = st_in[slot].astype(BF16)
            else:
                wb2[sl, :] = st_out[slot].astype(BF16)
                ff_chunk(x, f)

    @pl.when(jnp.logical_and(jnp.logical_not(fresh), used))
    def _():
        x = x_ref[...].astype(BF16)
        for f in range(n_f):
            ff_chunk(x, f)

    @pl.when(jnp.logical_not(used))
    def _():
        y_ref[...] = jnp.zeros(y_ref.shape, y_ref.dtype)


def _moe_experts(x_pad, blk_expert, n_used, w1, w3, w2):
    hbm = pl.BlockSpec(memory_space=pl.ANY)
    grid_spec = pltpu.PrefetchScalarGridSpec(
        num_scalar_prefetch=2,
        grid=(N_BLK,),
        in_specs=[pl.BlockSpec((MOE_BLK, D), lambda j, be, nu: (j, 0)), hbm, hbm, hbm],
        out_specs=pl.BlockSpec((MOE_BLK, D), lambda j, be, nu: (j, 0)),
        scratch_shapes=[
            pltpu.VMEM((D, D_FFE), BF16), pltpu.VMEM((D, D_FFE), BF16), pltpu.VMEM((D_FFE, D), BF16),
            pltpu.VMEM((2, D, FT), F32), pltpu.VMEM((2, FT, D), F32), pltpu.SemaphoreType.DMA((2,)),
        ],
    )
    return pl.pallas_call(
        _moe_kernel,
        grid_spec=grid_spec,
        out_shape=jax.ShapeDtypeStruct((N_PAD, D), F32),
        compiler_params=pltpu.CompilerParams(dimension_semantics=("arbitrary",), vmem_limit_bytes=VMEM_BIG),
        name="moe_experts",
    )(blk_expert, n_used, x_pad, w1, w3, w2)


def _combine_kernel(ya_ref, yb_ref, w_ref, h1_ref, mod_ref, g_ref, o_ref):
    f = w_ref[:, 0:1] * ya_ref[...] + w_ref[:, 1:2] * yb_ref[...]
    o_ref[...] = h1_ref[...] + mod_ref[0, 5:6, :] * _rms(f, g_ref[...])


def _moe_combine(yab, w, h1, mod_l, g):
    row = lambda i: (i, 0)
    return pl.pallas_call(
        _combine_kernel,
        grid=(NX // TM,),
        in_specs=[
            pl.BlockSpec((TM, D), row),
            pl.BlockSpec((TM, D), lambda i: (i + NX // TM, 0)),
            pl.BlockSpec((TM, LANE), row),
            pl.BlockSpec((TM, D), row),
            pl.BlockSpec((1, N_MOD, D), _mod_index(S // TM)),
            _const_spec((1, D)),
        ],
        out_specs=pl.BlockSpec((TM, D), row),
        out_shape=jax.ShapeDtypeStruct((NX, D), F32),
        compiler_params=pltpu.CompilerParams(dimension_semantics=("parallel",), vmem_limit_bytes=VMEM_BIG),
        name="moe_combine",
    )(yab, yab, w, h1, mod_l, g)


def _moe_layout(e, rank, counts):
    padded = (counts + MOE_BLK - 1) // MOE_BLK * MOE_BLK
    pad_end = jnp.cumsum(padded)
    pad_start = pad_end - padded
    onehot = e[:, :, None] == jnp.arange(N_EXP, dtype=jnp.int32)
    dest = jnp.sum(jnp.where(onehot, pad_start, 0), axis=-1) + rank
    blk_start = jnp.arange(N_BLK, dtype=jnp.int32) * MOE_BLK
    blk_expert = jnp.minimum(jnp.sum((blk_start[:, None] >= pad_end[None, :]).astype(jnp.int32), axis=1), N_EXP - 1)
    n_used = (pad_end[-1] // MOE_BLK).astype(jnp.int32).reshape(1)
    _, tok_sorted = lax.sort_key_val(dest.reshape(N_ASG), jnp.arange(N_ASG, dtype=jnp.int32) // 2)
    squeeze = pad_start - (jnp.cumsum(counts) - counts)
    slot = jnp.arange(N_PAD, dtype=jnp.int32)
    compact = slot - jnp.repeat(squeeze[blk_expert], MOE_BLK)
    slot_tok = tok_sorted.at[jnp.clip(compact, 0, N_ASG - 1)].get(mode="promise_in_bounds")
    return dest.astype(jnp.int32), slot_tok, blk_expert.astype(jnp.int32), n_used


def _rope_rot(w):
    half, quarter = ROPE // 2, ROPE // 4
    return jnp.concatenate([-w[..., quarter:half], w[..., :quarter], -w[..., half + quarter:], w[..., half:half + quarter]],
                           axis=-1)


def _rope_tables():
    half = ROPE // 2
    inv = ROPE_BASE ** (-(jnp.arange(0, half, 2, dtype=F32) / half))
    t = jnp.arange(S)
    ang_r = (t // GRID_W).astype(F32)[:, None] * inv
    ang_c = (t % GRID_W).astype(F32)[:, None] * inv
    ang = jnp.concatenate([ang_r, ang_r, ang_c, ang_c], axis=-1)
    cos32, sin32 = jnp.cos(ang), jnp.sin(ang)
    ones = lambda n: jnp.ones((S, n), F32)
    zeros = lambda n: jnp.zeros((S, n), F32)
    cos = jnp.concatenate([ones(NOPE), cos32, ones(LANE - QK)], axis=1)
    sin = jnp.concatenate([zeros(NOPE), sin32, zeros(LANE - QK)], axis=1)
    return (jnp.concatenate([cos, jnp.ones((TM, LANE), F32)], axis=0),
            jnp.concatenate([sin, jnp.zeros((TM, LANE), F32)], axis=0))


def _pad_heads(w, n_heads, width):
    w = w.reshape(w.shape[0], n_heads, width)
    return jnp.concatenate([w, jnp.zeros((w.shape[0], n_heads, LANE - width), w.dtype)], axis=2).reshape(w.shape[0], -1)


def _pack_layer(l, w_in, g_q, w_uq, g_kv, w_ukv, g_v, b_v, w_sp, b_sp, w_br, w_out):
    wi = w_in[l]
    offs = np.cumsum((0, Q_LORA, KV_LORA, ROPE, NA_W, NA_W, NA_W, CM_W, CM_W))
    cq, ckv, kr, naq, nak, nav, u, v = (wi[:, offs[i]:offs[i + 1]] for i in range(8))
    gl = wi[:, offs[8]:]
    zl = jnp.zeros((D, NOPE), F32)
    zr = jnp.zeros((D, LANE - NOPE - ROPE), F32)
    w_in_p = jnp.concatenate([cq, ckv, zl, kr, zr, zl, _rope_rot(kr), zr, naq, nak, _pad_heads(nav, H_NA, HD_NA),
                              u, v, gl], axis=1).astype(BF16)
    uq = w_uq[l].reshape(Q_LORA, H_MLA, QK)
    zq = jnp.zeros((Q_LORA, H_MLA, LANE - QK), F32)
    w_qa = jnp.concatenate([uq, zq], axis=2).reshape(Q_LORA, H_MLA * LANE).astype(BF16)
    w_qb = jnp.concatenate([jnp.zeros((Q_LORA, H_MLA, NOPE), F32), _rope_rot(uq[:, :, NOPE:]), zq], axis=2)
    w_qb = w_qb.reshape(Q_LORA, H_MLA * LANE).astype(BF16)
    ukv = w_ukv[l].reshape(KV_LORA, H_MLA, NOPE + VD)
    w_uk = _pad_heads(ukv[:, :, :NOPE].reshape(KV_LORA, -1), H_MLA, NOPE).astype(BF16)
    w_uv = _pad_heads(ukv[:, :, NOPE:].reshape(KV_LORA, -1), H_MLA, VD).astype(BF16)
    b_sp_t = jnp.repeat(b_sp[l].T, CM_W // CM_G, axis=1)
    return dict(
        w_in=w_in_p, g_q=g_q[l].reshape(1, -1), w_qa=w_qa, w_qb=w_qb, g_kv=g_kv[l].reshape(1, -1), w_uk=w_uk,
        w_uv=w_uv, g_v=g_v[l].reshape(1, -1), b_v=b_v[l].reshape(1, -1), w_sp=w_sp[l].astype(BF16), b_sp=b_sp_t,
        w_br=w_br[l].astype(BF16), w_out=w_out[l].astype(BF16))


def kernel(x, c, ctx, c_ctx, w_mod, b_mod, g_pre_mix, g_post_mix, g_pre_ffn, g_post_ffn, w_in, g_q, w_uq, g_kv, w_ukv, rpb, g_v, b_v, w_sp, b_sp, w_br, w_out, w_ffn1, w_ffn3, w_ffn2, w_router, w_moe1, w_moe3, w_moe2):
    assert DEPTH == 2, "layer 0 is the dense layer with context outputs, layer 1 the last (expert) layer"
    h_parts = (x.reshape(NX, D), ctx.reshape(NC, D))
    c_all = jnp.concatenate([c, c_ctx[None, :], jnp.zeros((16 - B - 1, D), F32)], axis=0)
    mod = _modulation(c_all, w_mod, b_mod).reshape(DEPTH, 16, N_MOD, D)
    rope_cos, rope_sin = _rope_tables()
    row1 = lambda a: a.reshape(1, -1)

    for l in range(DEPTH):
        last = l == DEPTH - 1
        wl = _pack_layer(l, w_in, g_q, w_uq, g_kv, w_ukv, g_v, b_v, w_sp, b_sp, w_br, w_out)
        mod_l = mod[l]
        q, k, v, naq, nak, nav, ycm, sg = _inproj(h_parts, mod_l, row1(g_pre_mix[l]), wl, rope_cos, rope_sin)
        ym = _mla_attention(q, k, v, with_ctx=not last)
        yn = _na_attention(naq, nak, nav, _na_bias_table(rpb[l]), with_ctx=not last)
        i = l // 2
        if not last:
            h = _merge_ffn(ym, yn, ycm, sg, h_parts, mod_l, row1(g_post_mix[l]), row1(g_pre_ffn[l]), wl,
                           row1(g_post_ffn[l]), w_ffn1[i].astype(BF16), w_ffn3[i].astype(BF16),
                           w_ffn2[i].astype(BF16))
            h_parts = (h, h)
        else:
            wr_hi = w_router[i].astype(BF16)
            wr_lo = (w_router[i] - wr_hi.astype(F32)).astype(BF16)
            zpad = jnp.zeros((D, LANE - N_EXP), BF16)
            wr = jnp.concatenate([wr_hi, zpad, wr_lo, zpad], axis=1)
            h1, t, e, w, cnt = _merge_route(ym, yn, ycm, sg, h_parts, mod_l, row1(g_post_mix[l]),
                                            row1(g_pre_ffn[l]), wl, wr)
            dest, slot_tok, blk_expert, n_used = _moe_layout(e[:, 0:2], e[:, 2:4], cnt[0, :N_EXP].astype(jnp.int32))
            inb = dict(mode="promise_in_bounds")
            x_pad = t.at[slot_tok].get(**inb)
            y = _moe_experts(x_pad, blk_expert, n_used, w_moe1[i], w_moe3[i], w_moe2[i])
            yab = y.at[dest.T.reshape(N_ASG)].get(**inb)
            h = _moe_combine(yab, w, h1, mod_l, row1(g_post_ffn[l]))
    return h[:NX].reshape(B, S, D)
```
